```python
import math
import jax, jax.numpy as jnp
from jax import lax
import numpy as np

D_MODEL = 1024
BATCH = 8
SEQ = 2048
DEPTH = 4

CHUNK = 64
Q_BLOCK = 128
N_MIXERS = 2
HEAD_DIM = 128
N_MEM = 256
MEM_HEADS = 4
MEM_WIDTH = MEM_HEADS * HEAD_DIM
MIX_WIDTH = 2 * D_MODEL
CONV_WIDTH = MIX_WIDTH - MEM_WIDTH
CONV_KERNEL = 31
ATT_HEADS = (MIX_WIDTH - MEM_WIDTH) // HEAD_DIM
ATT_WIDTH = ATT_HEADS * HEAD_DIM
IDX_HEADS = 8
IDX_DIM = 64
TOPK_MAX = 256
ALPHA = (2 * DEPTH) ** 0.25
BETA = (8 * DEPTH) ** -0.25
LN_EPS = 1e-5
NEG = -1e30

A_SIZES = (2 * CONV_WIDTH, MEM_WIDTH, MIX_WIDTH)
B_SIZES = (ATT_WIDTH, HEAD_DIM, HEAD_DIM, IDX_HEADS * IDX_DIM, IDX_DIM, IDX_HEADS, MEM_WIDTH, MIX_WIDTH)
A_IN = sum(A_SIZES)
B_IN = sum(B_SIZES)

kernel_name = "hybrid_conformer_conv_dsa_memory_deepnorm"


def split_cols(h, sizes):
    return jnp.split(h, [int(c) for c in np.cumsum(sizes)[:-1]], axis=-1)


def layer_norm(x, g, b):
    xf = x.astype(jnp.float32)
    mu = xf.mean(-1, keepdims=True)
    var = jnp.square(xf - mu).mean(-1, keepdims=True)
    return ((xf - mu) * lax.rsqrt(var + LN_EPS) * g.astype(jnp.float32) + b.astype(jnp.float32)).astype(x.dtype)


def alibi_slopes(n):
    p = 2 ** int(math.floor(math.log2(n)))
    base = [2.0 ** (-8.0 * (i + 1) / p) for i in range(p)]
    extra = [2.0 ** (-4.0 * (2 * i + 1) / p) for i in range(n - p)]
    return jnp.asarray(base + extra, dtype=jnp.float32)


def mem_cross_attention(q, mem_n, w_mkv):
    B, T, _ = q.shape
    k, v = jnp.split(mem_n @ w_mkv, 2, axis=-1)
    q = q.reshape(B, T, MEM_HEADS, HEAD_DIM)
    k = k.reshape(B, -1, MEM_HEADS, HEAD_DIM)
    v = v.reshape(B, -1, MEM_HEADS, HEAD_DIM)
    s = jnp.einsum('bthd,bnhd->bhtn', q, k).astype(jnp.float32) * (HEAD_DIM ** -0.5)
    p = jax.nn.softmax(s, axis=-1).astype(v.dtype)
    return jnp.einsum('bhtn,bnhd->bthd', p, v).reshape(B, T, MEM_WIDTH)


def conformer_conv(u, conv_w, conv_b, g, b):
    a, gl = jnp.split(u, 2, axis=-1)
    h = a * jax.nn.sigmoid(gl)
    h = jnp.pad(h, ((0, 0), (CONV_KERNEL - 1, 0), (0, 0)))
    h = lax.conv_general_dilated(h, conv_w[:, None, :].astype(h.dtype), (1,), 'VALID',
                                 dimension_numbers=('NWC', 'WIO', 'NWC'),
                                 feature_group_count=CONV_WIDTH) + conv_b
    return jax.nn.silu(layer_norm(h, g, b))


def dsa_attention(q, k, v, q_idx, k_idx, w_idx):
    B, T = q.shape[:2]
    topk = min(TOPK_MAX, T // 4)
    nb = T // Q_BLOCK
    slopes = alibi_slopes(ATT_HEADS)
    key_chunk = jnp.arange(T, dtype=jnp.int32) // CHUNK
    scale = HEAD_DIM ** -0.5

    def to_blocks(a):
        return jnp.moveaxis(a.reshape(B, nb, Q_BLOCK, *a.shape[2:]), 1, 0)

    def block(args):
        qb, qib, wb, start = args
        qpos = start + jnp.arange(Q_BLOCK, dtype=jnp.int32)
        qchunk = qpos // CHUNK
        admiss = key_chunk[None, :] <= qchunk[:, None]
        iscore = jax.nn.relu(jnp.einsum('bqhd,bsd->bqhs', qib, k_idx).astype(jnp.float32))
        iscore = jnp.einsum('bqh,bqhs->bqs', wb.astype(jnp.float32), iscore)
        iscore = jnp.where(admiss[None], iscore, NEG)
        _, sel = lax.top_k(iscore, topk)
        valid = key_chunk[sel] <= qchunk[None, :, None]
        kg = jax.vmap(lambda kb, ib: kb[ib])(k, sel)
        vg = jax.vmap(lambda vb, ib: vb[ib])(v, sel)
        dist = jnp.abs(qpos[None, :, None] - sel).astype(jnp.float32)
        logits = jnp.einsum('bqhd,bqkd->bqhk', qb, kg).astype(jnp.float32) * scale
        logits = logits - slopes[None, None, :, None] * dist[:, :, None, :]
        logits = jnp.where(valid[:, :, None, :], logits, NEG)
        p = jax.nn.softmax(logits, axis=-1).astype(vg.dtype)
        return jnp.einsum('bqhk,bqkd->bqhd', p, vg)

    starts = jnp.arange(nb, dtype=jnp.int32) * Q_BLOCK
    out = lax.map(block, (to_blocks(q), to_blocks(q_idx), to_blocks(w_idx), starts))
    return jnp.moveaxis(out, 0, 1).reshape(B, T, ATT_WIDTH)


def setup_inputs(seed: int = 0) -> dict:
    key = jax.random.key(seed)
    ks = jax.random.split(key, 24)
    n_a = (DEPTH + 1) // 2
    n_b = DEPTH // 2
    nrm = lambda k, shape, s: jax.random.normal(k, shape, jnp.float32) * s
    return {
        "x": nrm(ks[0], (BATCH, SEQ, D_MODEL), 1.0),
        "mem": nrm(ks[1], (BATCH, N_MEM, D_MODEL), 1.0),
        "mem_ln_g": 1.0 + nrm(ks[2], (D_MODEL,), 0.05),
        "mem_ln_b": nrm(ks[3], (D_MODEL,), 0.01),
        "a_w_in": nrm(ks[4], (n_a, D_MODEL, A_IN), D_MODEL ** -0.5),
        "a_conv_w": nrm(ks[5], (n_a, CONV_KERNEL, CONV_WIDTH), CONV_KERNEL ** -0.5),
        "a_conv_b": nrm(ks[6], (n_a, CONV_WIDTH), 0.01),
        "a_ln_g": 1.0 + nrm(ks[7], (n_a, CONV_WIDTH), 0.05),
        "a_ln_b": nrm(ks[8], (n_a, CONV_WIDTH), 0.01),
        "a_w_mkv": nrm(ks[9], (n_a, D_MODEL, 2 * MEM_WIDTH), D_MODEL ** -0.5),
        "a_w_out": nrm(ks[10], (n_a, MIX_WIDTH, D_MODEL), BETA * MIX_WIDTH ** -0.5),
        "a_post_g": 1.0 + nrm(ks[11], (n_a, D_MODEL), 0.05),
        "a_post_b": nrm(ks[12], (n_a, D_MODEL), 0.01),
        "b_w_in": nrm(ks[13], (n_b, D_MODEL, B_IN), D_MODEL ** -0.5),
        "b_w_mkv": nrm(ks[14], (n_b, D_MODEL, 2 * MEM_WIDTH), D_MODEL ** -0.5),
        "b_w_out": nrm(ks[15], (n_b, MIX_WIDTH, D_MODEL), BETA * MIX_WIDTH ** -0.5),
        "b_post_g": 1.0 + nrm(ks[16], (n_b, D_MODEL), 0.05),
        "b_post_b": nrm(ks[17], (n_b, D_MODEL), 0.01),
    }


def reference(x, mem, mem_ln_g, mem_ln_b, a_w_in, a_conv_w, a_conv_b, a_ln_g, a_ln_b,
              a_w_mkv, a_w_out, a_post_g, a_post_b, b_w_in, b_w_mkv, b_w_out,
              b_post_g, b_post_b):
    B, T, _ = x.shape
    mem_n = layer_norm(mem, mem_ln_g, mem_ln_b)
    for i in range(DEPTH):
        j = i // N_MIXERS
        if i % N_MIXERS == 0:
            u, qm, gate = split_cols(x @ a_w_in[j], A_SIZES)
            y_mix = conformer_conv(u, a_conv_w[j], a_conv_b[j], a_ln_g[j], a_ln_b[j])
            w_mkv, w_out, pg, pb = a_w_mkv[j], a_w_out[j], a_post_g[j], a_post_b[j]
        else:
            q, k, v, qi, ki, wi, qm, gate = split_cols(x @ b_w_in[j], B_SIZES)
            y_mix = dsa_attention(q.reshape(B, T, ATT_HEADS, HEAD_DIM), k, v,
                                  qi.reshape(B, T, IDX_HEADS, IDX_DIM), ki, wi)
            w_mkv, w_out, pg, pb = b_w_mkv[j], b_w_out[j], b_post_g[j], b_post_b[j]
        y_mem = mem_cross_attention(qm, mem_n, w_mkv)
        y = jnp.concatenate([y_mix, y_mem], axis=-1) * jax.nn.silu(gate)
        x = layer_norm(ALPHA * x + y @ w_out, pg, pb)
    return x
```

```python
import functools
import math

import numpy as np
import jax
import jax.numpy as jnp
from jax import lax
from jax.experimental import pallas as pl
from jax.experimental.pallas import tpu as pltpu

D_MODEL = 1024
DEPTH = 4
CHUNK = 64
Q_BLOCK = 128
HEAD_DIM = 128
N_MEM = 256
MEM_HEADS = 4
MEM_WIDTH = MEM_HEADS * HEAD_DIM
MIX_WIDTH = 2 * D_MODEL
CONV_WIDTH = MIX_WIDTH - MEM_WIDTH
CONV_KERNEL = 31
ATT_HEADS = CONV_WIDTH // HEAD_DIM
ATT_WIDTH = ATT_HEADS * HEAD_DIM
IDX_HEADS = 8
IDX_DIM = 64
TOPK_MAX = 256


def _f32(v):
    return float(np.float32(v))


ALPHA = _f32((2 * DEPTH) ** 0.25)
LN_EPS = _f32(1e-5)
NEG = _f32(-1e30)
SCALE = _f32(HEAD_DIM ** -0.5)

LANES = 128
CONV_HALO = 32
VMEM_LIMIT = 56 * 1024 * 1024

BF16 = jnp.bfloat16
F32 = jnp.float32


def _alibi_slopes(n):
    p = 2 ** int(math.floor(math.log2(n)))
    base = [2.0 ** (-8.0 * (i + 1) / p) for i in range(p)]
    extra = [2.0 ** (-4.0 * (2 * i + 1) / p) for i in range(n - p)]
    return [_f32(s) for s in base + extra]


SLOPES = _alibi_slopes(ATT_HEADS)


def _ln(x, g, b):
    mu = jnp.mean(x, axis=-1, keepdims=True)
    xc = x - mu
    var = jnp.mean(xc * xc, axis=-1, keepdims=True)
    return xc * lax.rsqrt(var + LN_EPS) * g + b


def _sigmoid(x):
    return 1.0 / (1.0 + jnp.exp(-x))


def _dot(a, b):
    return jnp.dot(a, b, preferred_element_type=F32)


def _dot_nt(a, b):
    return lax.dot_general(a, b, (((1,), (1,)), ((), ())), preferred_element_type=F32)


def _const_spec(shape):
    nd = len(shape)
    return pl.BlockSpec(shape, lambda *_: (0,) * nd, pipeline_mode=pl.Buffered(1))


def _memkv_kernel(mem_ref, g_ref, b_ref, w_ref, kv_ref):
    m = _ln(mem_ref[...], g_ref[...], b_ref[...])
    kv_ref[...] = _dot(m.astype(BF16), w_ref[...]).astype(BF16)


def _memkv(mem, g, b, w_all):
    nb = mem.shape[0]
    return pl.pallas_call(
        _memkv_kernel,
        grid=(DEPTH, nb),
        in_specs=[
            pl.BlockSpec((None, N_MEM, D_MODEL), lambda l, i: (i, 0, 0)),
            pl.BlockSpec((1, D_MODEL), lambda l, i: (0, 0)),
            pl.BlockSpec((1, D_MODEL), lambda l, i: (0, 0)),
            pl.BlockSpec((None, D_MODEL, 2 * MEM_WIDTH), lambda l, i: (l, 0, 0)),
        ],
        out_specs=pl.BlockSpec((None, None, N_MEM, 2 * MEM_WIDTH), lambda l, i: (l, i, 0, 0)),
        out_shape=jax.ShapeDtypeStruct((DEPTH, nb, N_MEM, 2 * MEM_WIDTH), BF16),
        compiler_params=pltpu.CompilerParams(dimension_semantics=("arbitrary", "arbitrary")),
        name="memkv",
    )(mem, g, b, w_all)


def _mem_attn(qm, kv_ref):
    outs = []
    for h in range(MEM_HEADS):
        lo = h * HEAD_DIM
        qh = (qm[:, lo:lo + HEAD_DIM] * SCALE).astype(BF16)
        kh = kv_ref[:, lo:lo + HEAD_DIM]
        vh = kv_ref[:, MEM_WIDTH + lo:MEM_WIDTH + lo + HEAD_DIM]
        s = _dot_nt(qh, kh)
        m = jnp.max(s, axis=-1, keepdims=True)
        p = jnp.exp(s - m)
        l = jnp.sum(p, axis=-1, keepdims=True)
        outs.append(_dot(p.astype(BF16), vh) / l)
    return outs


def _gated_out(x, y_mix, y_mem_heads, gate, wout_ref, pg_ref, pb_ref):
    sg = gate * _sigmoid(gate)
    acc = _dot((y_mix * sg[:, :CONV_WIDTH]).astype(BF16), wout_ref[0:CONV_WIDTH, :])
    for h, yh in enumerate(y_mem_heads):
        lo = CONV_WIDTH + h * HEAD_DIM
        acc = acc + _dot((yh * sg[:, lo:lo + HEAD_DIM]).astype(BF16), wout_ref[lo:lo + HEAD_DIM, :])
    return _ln(ALPHA * x + acc, pg_ref[...], pb_ref[...])


def _layer_a_kernel(x_ref, win_ref, cw_ref, cb_ref, lg_ref, lb_ref, kv_ref, wout_ref, pg_ref, pb_ref,
                    o_ref, hbuf, cbuf, *, tt):
    t = pl.program_id(1)
    x = x_ref[...]
    xb = x.astype(BF16)
    a = _dot(xb, win_ref[:, 0:CONV_WIDTH])
    g = _dot(xb, win_ref[:, CONV_WIDTH:2 * CONV_WIDTH])

    @pl.when(t == 0)
    def _():
        hbuf[0:CONV_HALO, :] = jnp.zeros((CONV_HALO, CONV_WIDTH), F32)

    @pl.when(t > 0)
    def _():
        hbuf[0:CONV_HALO, :] = hbuf[tt:tt + CONV_HALO, :]

    hbuf[CONV_HALO:CONV_HALO + tt, :] = a * _sigmoid(g)

    rc = 64
    off = CONV_HALO - (CONV_KERNEL - 1)

    def conv_cols(c, carry):
        col = pl.multiple_of(c * LANES, LANES)
        for r in range(tt // rc):
            acc = jnp.broadcast_to(cb_ref[:, pl.ds(col, LANES)], (rc, LANES))
            for k in range(CONV_KERNEL):
                w = cw_ref[pl.ds(k, 1), pl.ds(col, LANES)]
                acc = acc + hbuf[r * rc + off + k:r * rc + off + k + rc, pl.ds(col, LANES)] * w
            cbuf[r * rc:(r + 1) * rc, pl.ds(col, LANES)] = acc
        return carry

    lax.fori_loop(0, CONV_WIDTH // LANES, conv_cols, 0)

    yn = _ln(cbuf[...], lg_ref[...], lb_ref[...])
    y_mix = yn * _sigmoid(yn)

    qm = _dot(xb, win_ref[:, 2 * CONV_WIDTH:2 * CONV_WIDTH + MEM_WIDTH])
    y_mem = _mem_attn(qm, kv_ref)
    gate = _dot(xb, win_ref[:, 2 * CONV_WIDTH + MEM_WIDTH:])
    o_ref[...] = _gated_out(x, y_mix, y_mem, gate, wout_ref, pg_ref, pb_ref)


def _layer_a(x, layer, w_in, conv_w, conv_b, ln_g, ln_b, kv, w_out, pg, pb, *, tt=256):
    nb, seq, _ = x.shape
    a_in = w_in.shape[1]
    return pl.pallas_call(
        functools.partial(_layer_a_kernel, tt=tt),
        grid=(nb, seq // tt),
        in_specs=[
            pl.BlockSpec((None, tt, D_MODEL), lambda b, t: (b, t, 0)),
            _const_spec((D_MODEL, a_in)),
            _const_spec((CONV_KERNEL + 1, CONV_WIDTH)),
            _const_spec((1, CONV_WIDTH)),
            _const_spec((1, CONV_WIDTH)),
            _const_spec((1, CONV_WIDTH)),
            pl.BlockSpec((None, None, N_MEM, 2 * MEM_WIDTH), lambda b, t: (layer, b, 0, 0)),
            _const_spec((MIX_WIDTH, D_MODEL)),
            _const_spec((1, D_MODEL)),
            _const_spec((1, D_MODEL)),
        ],
        out_specs=pl.BlockSpec((None, tt, D_MODEL), lambda b, t: (b, t, 0)),
        out_shape=jax.ShapeDtypeStruct(x.shape, F32),
        scratch_shapes=[
            pltpu.VMEM((tt + CONV_HALO, CONV_WIDTH), F32),
            pltpu.VMEM((tt, CONV_WIDTH), F32),
        ],
        compiler_params=pltpu.CompilerParams(
            dimension_semantics=("arbitrary", "arbitrary"), vmem_limit_bytes=VMEM_LIMIT),
        name="layer_a",
    )(x, w_in, conv_w, conv_b, ln_g, ln_b, kv, w_out, pg, pb)


IDX_PAD = IDX_HEADS * IDX_DIM + 2 * LANES


def _b_proj_kernel(x_ref, wqkv_ref, widx_ref, q_ref, k_ref, v_ref, qi_ref, ki_ref, wi_ref):
    xb = x_ref[...].astype(BF16)
    qkv = _dot(xb, wqkv_ref[...])
    q_ref[...] = (qkv[:, :ATT_WIDTH] * SCALE).astype(BF16)
    k_ref[...] = qkv[:, ATT_WIDTH:ATT_WIDTH + HEAD_DIM].astype(BF16)
    v_ref[...] = qkv[:, ATT_WIDTH + HEAD_DIM:].astype(BF16)
    idx = _dot(xb, widx_ref[...])
    nq = IDX_HEADS * IDX_DIM
    qi_ref[...] = idx[:, :nq].astype(BF16)
    ki_ref[...] = idx[:, nq:nq + IDX_DIM].astype(BF16)
    wi_ref[...] = idx[:, nq + LANES:nq + LANES + IDX_HEADS]


def _b_proj(x, wqkv, widx, *, tt=512):
    nb, seq, _ = x.shape
    nq = IDX_HEADS * IDX_DIM

    def tile(w):
        return pl.BlockSpec((None, tt, w), lambda b, t: (b, t, 0))

    return pl.pallas_call(
        _b_proj_kernel,
        grid=(nb, seq // tt),
        in_specs=[tile(D_MODEL), _const_spec(wqkv.shape), _const_spec(widx.shape)],
        out_specs=[tile(ATT_WIDTH), tile(HEAD_DIM), tile(HEAD_DIM), tile(nq), tile(IDX_DIM), tile(IDX_HEADS)],
        out_shape=[
            jax.ShapeDtypeStruct((nb, seq, ATT_WIDTH), BF16),
            jax.ShapeDtypeStruct((nb, seq, HEAD_DIM), BF16),
            jax.ShapeDtypeStruct((nb, seq, HEAD_DIM), BF16),
            jax.ShapeDtypeStruct((nb, seq, nq), BF16),
            jax.ShapeDtypeStruct((nb, seq, IDX_DIM), BF16),
            jax.ShapeDtypeStruct((nb, seq, IDX_HEADS), F32),
        ],
        compiler_params=pltpu.CompilerParams(
            dimension_semantics=("arbitrary", "arbitrary"), vmem_limit_bytes=VMEM_LIMIT),
        name="b_proj",
    )(x, wqkv, widx)


def _dsa_kernel(q_ref, k_ref, v_ref, qi_ref, ki_ref, wi_ref, o_ref, key_ref, *, seq, topk):
    j = pl.program_id(1)
    qb = Q_BLOCK
    row = lax.broadcasted_iota(jnp.int32, (qb, seq), 0)
    col = lax.broadcasted_iota(jnp.int32, (qb, seq), 1)
    qpos = row + j * qb
    admiss = (col >> 6) <= (qpos >> 6)

    ki = ki_ref[...]
    wi = wi_ref[...]
    isc = jnp.zeros((qb, seq), F32)
    for h in range(IDX_HEADS):
        s = _dot_nt(qi_ref[:, h * IDX_DIM:(h + 1) * IDX_DIM], ki)
        isc = isc + wi[:, h:h + 1] * jnp.maximum(s, 0.0)
    isc = jnp.where(admiss, isc, NEG) + 0.0

    bits = pltpu.bitcast(isc, jnp.int32)
    key_ref[...] = jnp.where(bits < 0, bits ^ jnp.int32(0x7FFFFFFF), bits)

    kf = F32(topk)

    def tbody(i, lo):
        cand = lo + (jnp.int32(1) << (31 - i))
        cnt = jnp.sum(jnp.where(key_ref[...] >= cand, 1.0, 0.0), axis=-1, keepdims=True)
        return jnp.where(cnt >= kf, cand, lo)

    thr = lax.fori_loop(0, 32, tbody, jnp.full((qb, 1), jnp.iinfo(jnp.int32).min, jnp.int32))

    keys = key_ref[...]
    gt = keys > thr
    eq = keys == thr
    need = kf - jnp.sum(jnp.where(gt, 1.0, 0.0), axis=-1, keepdims=True)

    def cbody(i, cut):
        cand = cut + (jnp.int32(1) << (11 - i))
        cnt = jnp.sum(jnp.where(eq & (col < cand), 1.0, 0.0), axis=-1, keepdims=True)
        return jnp.where((cnt <= need) & (cand <= seq), cand, cut)

    cut = lax.fori_loop(0, 12, cbody, jnp.zeros((qb, 1), jnp.int32))
    mask = admiss & (gt | (eq & (col < cut)))

    dist = jnp.abs(qpos - col).astype(F32)
    k = k_ref[...]
    v = v_ref[...]
    for h in range(ATT_HEADS):
        lo = h * HEAD_DIM
        logits = _dot_nt(q_ref[:, lo:lo + HEAD_DIM], k) - SLOPES[h] * dist
        logits = jnp.where(mask, logits, NEG)
        m = jnp.max(logits, axis=-1, keepdims=True)
        p = jnp.exp(logits - m)
        l = jnp.sum(p, axis=-1, keepdims=True)
        o_ref[:, lo:lo + HEAD_DIM] = _dot(p.astype(BF16), v) / l


def _dsa(q, k, v, qi, ki, wi):
    nb, seq, _ = q.shape
    topk = min(TOPK_MAX, seq // 4)
    assert seq <= 2048 and seq % Q_BLOCK == 0 and CHUNK == 64

    def qtile(w):
        return pl.BlockSpec((None, Q_BLOCK, w), lambda b, j: (b, j, 0))

    def full(w):
        return pl.BlockSpec((None, seq, w), lambda b, j: (b, 0, 0))

    return pl.pallas_call(
        functools.partial(_dsa_kernel, seq=seq, topk=topk),
        grid=(nb, seq // Q_BLOCK),
        in_specs=[qtile(ATT_WIDTH), full(HEAD_DIM), full(HEAD_DIM),
                  qtile(IDX_HEADS * IDX_DIM), full(IDX_DIM), qtile(IDX_HEADS)],
        out_specs=qtile(ATT_WIDTH),
        out_shape=jax.ShapeDtypeStruct((nb, seq, ATT_WIDTH), F32),
        scratch_shapes=[pltpu.VMEM((Q_BLOCK, seq), jnp.int32)],
        compiler_params=pltpu.CompilerParams(
            dimension_semantics=("arbitrary", "arbitrary"), vmem_limit_bytes=VMEM_LIMIT),
        name="dsa",
    )(q, k, v, qi, ki, wi)


def _b_out_kernel(x_ref, ymix_ref, wqm_ref, wgate_ref, kv_ref, wout_ref, pg_ref, pb_ref, o_ref):
    x = x_ref[...]
    xb = x.astype(BF16)
    y_mem = _mem_attn(_dot(xb, wqm_ref[...]), kv_ref)
    gate = _dot(xb, wgate_ref[...])
    o_ref[...] = _gated_out(x, ymix_ref[...], y_mem, gate, wout_ref, pg_ref, pb_ref)


def _b_out(x, y_mix, layer, wqm, wgate, kv, w_out, pg, pb, *, tt=256):
    nb, seq, _ = x.shape
    return pl.pallas_call(
        _b_out_kernel,
        grid=(nb, seq // tt),
        in_specs=[
            pl.BlockSpec((None, tt, D_MODEL), lambda b, t: (b, t, 0)),
            pl.BlockSpec((None, tt, ATT_WIDTH), lambda b, t: (b, t, 0)),
            _const_spec(wqm.shape),
            _const_spec(wgate.shape),
            pl.BlockSpec((None, None, N_MEM, 2 * MEM_WIDTH), lambda b, t: (layer, b, 0, 0)),
            _const_spec((MIX_WIDTH, D_MODEL)),
            _const_spec((1, D_MODEL)),
            _const_spec((1, D_MODEL)),
        ],
        out_specs=pl.BlockSpec((None, tt, D_MODEL), lambda b, t: (b, t, 0)),
        out_shape=jax.ShapeDtypeStruct(x.shape, F32),
        compiler_params=pltpu.CompilerParams(
            dimension_semantics=("arbitrary", "arbitrary"), vmem_limit_bytes=VMEM_LIMIT),
        name="b_out",
    )(x, y_mix, wqm, wgate, kv, w_out, pg, pb)


def _pad_cols(w, n):
    return jnp.pad(w, ((0, 0), (0, n - w.shape[1])))


def kernel(x, mem, mem_ln_g, mem_ln_b, a_w_in, a_conv_w, a_conv_b, a_ln_g, a_ln_b, a_w_mkv, a_w_out,
           a_post_g, a_post_b, b_w_in, b_w_mkv, b_w_out, b_post_g, b_post_b):
    row = lambda v: v.reshape(1, -1)
    w_mkv_all = jnp.stack([(a_w_mkv if i % 2 == 0 else b_w_mkv)[i // 2] for i in range(DEPTH)]).astype(BF16)
    kv = _memkv(mem, row(mem_ln_g), row(mem_ln_b), w_mkv_all)

    c0 = ATT_WIDTH + 2 * HEAD_DIM
    c1 = c0 + IDX_HEADS * IDX_DIM
    c2 = c1 + IDX_DIM
    c3 = c2 + IDX_HEADS
    c4 = c3 + MEM_WIDTH

    for i in range(DEPTH):
        j = i // 2
        if i % 2 == 0:
            x = _layer_a(
                x, i, a_w_in[j].astype(BF16),
                jnp.pad(a_conv_w[j], ((0, 1), (0, 0))), row(a_conv_b[j]), row(a_ln_g[j]), row(a_ln_b[j]),
                kv, a_w_out[j].astype(BF16), row(a_post_g[j]), row(a_post_b[j]))
        else:
            w = b_w_in[j]
            wqkv = w[:, :c0].astype(BF16)
            widx = jnp.concatenate(
                [w[:, c0:c1], _pad_cols(w[:, c1:c2], LANES), _pad_cols(w[:, c2:c3], LANES)], axis=1).astype(BF16)
            q, k, v, qi, ki, wi = _b_proj(x, wqkv, widx)
            y_mix = _dsa(q, k, v, qi, ki, wi)
            x = _b_out(x, y_mix, i, w[:, c3:c4].astype(BF16), w[:, c4:].astype(BF16), kv,
                       b_w_out[j].astype(BF16), row(b_post_g[j]), row(b_post_b[j]))
    return x
```

```python
import functools
import math

import numpy as np
import jax
import jax.numpy as jnp
from jax import lax
from jax.experimental import pallas as pl
from jax.experimental.pallas import tpu as pltpu

D_MODEL = 1024
DEPTH = 4
CHUNK = 64
Q_BLOCK = 128
HEAD_DIM = 128
N_MEM = 256
MEM_HEADS = 4
MEM_WIDTH = MEM_HEADS * HEAD_DIM
MIX_WIDTH = 2 * D_MODEL
CONV_WIDTH = MIX_WIDTH - MEM_WIDTH
CONV_KERNEL = 31
ATT_HEADS = CONV_WIDTH // HEAD_DIM
ATT_WIDTH = ATT_HEADS * HEAD_DIM
IDX_HEADS = 8
IDX_DIM = 64
TOPK_MAX = 256


def _f32(v):
    return float(np.float32(v))


ALPHA = _f32((2 * DEPTH) ** 0.25)
LN_EPS = _f32(1e-5)
NEG = _f32(-1e30)
SCALE = _f32(HEAD_DIM ** -0.5)

LANES = 128
CONV_HALO = 32
VMEM_LIMIT = 56 * 1024 * 1024

BF16 = jnp.bfloat16
F32 = jnp.float32


def _alibi_slopes(n):
    p = 2 ** int(math.floor(math.log2(n)))
    base = [2.0 ** (-8.0 * (i + 1) / p) for i in range(p)]
    extra = [2.0 ** (-4.0 * (2 * i + 1) / p) for i in range(n - p)]
    return [_f32(s) for s in base + extra]


SLOPES = _alibi_slopes(ATT_HEADS)


def _ln(x, g, b):
    mu = jnp.mean(x, axis=-1, keepdims=True)
    xc = x - mu
    var = jnp.mean(xc * xc, axis=-1, keepdims=True)
    return xc * lax.rsqrt(var + LN_EPS) * g + b


def _sigmoid(x):
    return 1.0 / (1.0 + jnp.exp(-x))


def _dot(a, b):
    return jnp.dot(a, b, preferred_element_type=F32)


def _dot_nt(a, b):
    return lax.dot_general(a, b, (((1,), (1,)), ((), ())), preferred_element_type=F32)


def _const_spec(shape):
    nd = len(shape)
    return pl.BlockSpec(shape, lambda *_: (0,) * nd, pipeline_mode=pl.Buffered(1))


def _memkv_kernel(mem_ref, g_ref, b_ref, w_ref, kv_ref):
    m = _ln(mem_ref[...], g_ref[...], b_ref[...])
    kv_ref[...] = _dot(m.astype(BF16), w_ref[...]).astype(BF16)


def _memkv(mem, g, b, w_all):
    nb = mem.shape[0]
    return pl.pallas_call(
        _memkv_kernel,
        grid=(DEPTH, nb),
        in_specs=[
            pl.BlockSpec((None, N_MEM, D_MODEL), lambda l, i: (i, 0, 0)),
            pl.BlockSpec((1, D_MODEL), lambda l, i: (0, 0)),
            pl.BlockSpec((1, D_MODEL), lambda l, i: (0, 0)),
            pl.BlockSpec((None, D_MODEL, 2 * MEM_WIDTH), lambda l, i: (l, 0, 0)),
        ],
        out_specs=pl.BlockSpec((None, None, N_MEM, 2 * MEM_WIDTH), lambda l, i: (l, i, 0, 0)),
        out_shape=jax.ShapeDtypeStruct((DEPTH, nb, N_MEM, 2 * MEM_WIDTH), BF16),
        compiler_params=pltpu.CompilerParams(dimension_semantics=("arbitrary", "arbitrary")),
        name="memkv",
    )(mem, g, b, w_all)


def _mem_attn(qm, kv_ref):
    outs = []
    for h in range(MEM_HEADS):
        lo = h * HEAD_DIM
        qh = (qm[:, lo:lo + HEAD_DIM] * SCALE).astype(BF16)
        kh = kv_ref[:, lo:lo + HEAD_DIM]
        vh = kv_ref[:, MEM_WIDTH + lo:MEM_WIDTH + lo + HEAD_DIM]
        s = _dot_nt(qh, kh)
        m = jnp.max(s, axis=-1, keepdims=True)
        p = jnp.exp(s - m)
        l = jnp.sum(p, axis=-1, keepdims=True)
        outs.append(_dot(p.astype(BF16), vh) / l)
    return outs


def _gated_out(x, y_mix, y_mem_heads, gate, wout_ref, pg_ref, pb_ref):
    sg = gate * _sigmoid(gate)
    acc = _dot((y_mix * sg[:, :CONV_WIDTH]).astype(BF16), wout_ref[0:CONV_WIDTH, :])
    for h, yh in enumerate(y_mem_heads):
        lo = CONV_WIDTH + h * HEAD_DIM
        acc = acc + _dot((yh * sg[:, lo:lo + HEAD_DIM]).astype(BF16), wout_ref[lo:lo + HEAD_DIM, :])
    return _ln(ALPHA * x + acc, pg_ref[...], pb_ref[...])


def _layer_a_kernel(x_ref, win_ref, cw_ref, cb_ref, lg_ref, lb_ref, kv_ref, wout_ref, pg_ref, pb_ref,
                    o_ref, hbuf, cbuf, *, tt):
    t = pl.program_id(1)
    x = x_ref[...]
    xb = x.astype(BF16)
    a = _dot(xb, win_ref[:, 0:CONV_WIDTH])
    g = _dot(xb, win_ref[:, CONV_WIDTH:2 * CONV_WIDTH])

    @pl.when(t == 0)
    def _():
        hbuf[0:CONV_HALO, :] = jnp.zeros((CONV_HALO, CONV_WIDTH), F32)

    @pl.when(t > 0)
    def _():
        hbuf[0:CONV_HALO, :] = hbuf[tt:tt + CONV_HALO, :]

    hbuf[CONV_HALO:CONV_HALO + tt, :] = a * _sigmoid(g)

    rc = 64
    off = CONV_HALO - (CONV_KERNEL - 1)

    def conv_cols(c, carry):
        col = pl.multiple_of(c * LANES, LANES)
        for r in range(tt // rc):
            acc = jnp.broadcast_to(cb_ref[:, pl.ds(col, LANES)], (rc, LANES))
            for k in range(CONV_KERNEL):
                w = cw_ref[pl.ds(k, 1), pl.ds(col, LANES)]
                acc = acc + hbuf[r * rc + off + k:r * rc + off + k + rc, pl.ds(col, LANES)] * w
            cbuf[r * rc:(r + 1) * rc, pl.ds(col, LANES)] = acc
        return carry

    lax.fori_loop(0, CONV_WIDTH // LANES, conv_cols, 0)

    yn = _ln(cbuf[...], lg_ref[...], lb_ref[...])
    y_mix = yn * _sigmoid(yn)

    qm = _dot(xb, win_ref[:, 2 * CONV_WIDTH:2 * CONV_WIDTH + MEM_WIDTH])
    y_mem = _mem_attn(qm, kv_ref)
    gate = _dot(xb, win_ref[:, 2 * CONV_WIDTH + MEM_WIDTH:])
    o_ref[...] = _gated_out(x, y_mix, y_mem, gate, wout_ref, pg_ref, pb_ref)


def _layer_a(x, layer, w_in, conv_w, conv_b, ln_g, ln_b, kv, w_out, pg, pb, *, tt=256):
    nb, seq, _ = x.shape
    a_in = w_in.shape[1]
    return pl.pallas_call(
        functools.partial(_layer_a_kernel, tt=tt),
        grid=(nb, seq // tt),
        in_specs=[
            pl.BlockSpec((None, tt, D_MODEL), lambda b, t: (b, t, 0)),
            _const_spec((D_MODEL, a_in)),
            _const_spec((CONV_KERNEL + 1, CONV_WIDTH)),
            _const_spec((1, CONV_WIDTH)),
            _const_spec((1, CONV_WIDTH)),
            _const_spec((1, CONV_WIDTH)),
            pl.BlockSpec((None, None, N_MEM, 2 * MEM_WIDTH), lambda b, t: (layer, b, 0, 0)),
            _const_spec((MIX_WIDTH, D_MODEL)),
            _const_spec((1, D_MODEL)),
            _const_spec((1, D_MODEL)),
        ],
        out_specs=pl.BlockSpec((None, tt, D_MODEL), lambda b, t: (b, t, 0)),
        out_shape=jax.ShapeDtypeStruct(x.shape, F32),
        scratch_shapes=[
            pltpu.VMEM((tt + CONV_HALO, CONV_WIDTH), F32),
            pltpu.VMEM((tt, CONV_WIDTH), F32),
        ],
        compiler_params=pltpu.CompilerParams(
            dimension_semantics=("arbitrary", "arbitrary"), vmem_limit_bytes=VMEM_LIMIT),
        name="layer_a",
    )(x, w_in, conv_w, conv_b, ln_g, ln_b, kv, w_out, pg, pb)


IDX_PAD = IDX_HEADS * IDX_DIM + 2 * LANES


def _b_proj_kernel(x_ref, wqkv_ref, widx_ref, q_ref, k_ref, v_ref, qi_ref, ki_ref, wi_ref):
    xb = x_ref[...].astype(BF16)
    qkv = _dot(xb, wqkv_ref[...])
    q_ref[...] = (qkv[:, :ATT_WIDTH] * SCALE).astype(BF16)
    k_ref[...] = qkv[:, ATT_WIDTH:ATT_WIDTH + HEAD_DIM].astype(BF16)
    v_ref[...] = qkv[:, ATT_WIDTH + HEAD_DIM:].astype(BF16)
    idx = _dot(xb, widx_ref[...])
    nq = IDX_HEADS * IDX_DIM
    qi_ref[...] = idx[:, :nq].astype(BF16)
    ki_ref[...] = idx[:, nq:nq + IDX_DIM].astype(BF16)
    wi_ref[...] = idx[:, nq + LANES:nq + LANES + IDX_HEADS]


def _b_proj(x, wqkv, widx, *, tt=512):
    nb, seq, _ = x.shape
    nq = IDX_HEADS * IDX_DIM

    def tile(w):
        return pl.BlockSpec((None, tt, w), lambda b, t: (b, t, 0))

    return pl.pallas_call(
        _b_proj_kernel,
        grid=(nb, seq // tt),
        in_specs=[tile(D_MODEL), _const_spec(wqkv.shape), _const_spec(widx.shape)],
        out_specs=[tile(ATT_WIDTH), tile(HEAD_DIM), tile(HEAD_DIM), tile(nq), tile(IDX_DIM), tile(IDX_HEADS)],
        out_shape=[
            jax.ShapeDtypeStruct((nb, seq, ATT_WIDTH), BF16),
            jax.ShapeDtypeStruct((nb, seq, HEAD_DIM), BF16),
            jax.ShapeDtypeStruct((nb, seq, HEAD_DIM), BF16),
            jax.ShapeDtypeStruct((nb, seq, nq), BF16),
            jax.ShapeDtypeStruct((nb, seq, IDX_DIM), BF16),
            jax.ShapeDtypeStruct((nb, seq, IDX_HEADS), F32),
        ],
        compiler_params=pltpu.CompilerParams(
            dimension_semantics=("arbitrary", "arbitrary"), vmem_limit_bytes=VMEM_LIMIT),
        name="b_proj",
    )(x, wqkv, widx)


NEG_KEY = int(np.float32(NEG).view(np.int32)) ^ 0x7FFFFFFF
MASKED_DIST = _f32(-NEG / min(SLOPES))
KEY_TILE = 256
SEARCH_GROUPS = 4


def _count(pred):
    return jnp.sum(jnp.where(pred, 1.0, 0.0), axis=-1, keepdims=True)


def _dsa_block(s, j, q_ref, k_ref, v_ref, qi_ref, ki_ref, wi_ref, slopes_ref, o_ref, key_ref, d_ref, *, topk):
    qb = Q_BLOCK
    row = lax.broadcasted_iota(jnp.int32, (qb, s), 0)
    col = lax.broadcasted_iota(jnp.int32, (qb, s), 1)
    qpos = row + j * qb
    admiss = (col >> 6) <= (qpos >> 6)

    ki = ki_ref[0:s, :]
    wi = wi_ref[...]
    isc = jnp.zeros((qb, s), F32)
    for h in range(IDX_HEADS):
        sc = _dot_nt(qi_ref[:, h * IDX_DIM:(h + 1) * IDX_DIM], ki)
        isc = isc + wi[:, h:h + 1] * jnp.maximum(sc, 0.0)
    isc = jnp.where(admiss, isc, NEG) + 0.0

    bits = pltpu.bitcast(isc, jnp.int32)
    key_ref[:, 0:s] = jnp.where(bits < 0, bits ^ jnp.int32(0x7FFFFFFF), bits)

    kf = F32(topk)
    rg = qb // SEARCH_GROUPS

    def tbody(i, los):
        step = jnp.int32(1) << (31 - i)
        out = []
        for g, lo in enumerate(los):
            cand = lo + step
            cnt = _count(key_ref[g * rg:(g + 1) * rg, 0:s] >= cand)
            out.append(jnp.where(cnt >= kf, cand, lo))
        return tuple(out)

    lo0 = jnp.full((rg, 1), jnp.iinfo(jnp.int32).min, jnp.int32)
    thr = jnp.concatenate(lax.fori_loop(0, 32, tbody, (lo0,) * SEARCH_GROUPS), axis=0)

    keys = key_ref[:, 0:s]
    n_ge = _count(keys >= thr)
    dist = jnp.abs(qpos - col).astype(F32)
    has_tie = jnp.max(jnp.where((n_ge > kf) & (thr > NEG_KEY), 1.0, 0.0)) > 0.0

    @pl.when(jnp.logical_not(has_tie))
    def _():
        d_ref[:, 0:s] = jnp.where(admiss, jnp.where(keys >= thr, dist, MASKED_DIST), MASKED_DIST)

    @pl.when(has_tie)
    def _():
        eq = keys == thr
        need = kf - _count(keys > thr)

        def cbody(i, cut):
            cand = cut + (jnp.int32(1) << (11 - i))
            cnt = _count(eq & (col < cand))
            return jnp.where((cnt <= need) & (cand <= s), cand, cut)

        cut = lax.fori_loop(0, 12, cbody, jnp.zeros((qb, 1), jnp.int32))
        sel = (keys > thr) | (eq & (col < cut))
        d_ref[:, 0:s] = jnp.where(admiss, jnp.where(sel, dist, MASKED_DIST), MASKED_DIST)

    def hbody(h, carry):
        lo = pl.multiple_of(h * HEAD_DIM, HEAD_DIM)
        logits = _dot_nt(q_ref[:, pl.ds(lo, HEAD_DIM)], k_ref[0:s, :]) - slopes_ref[h] * d_ref[:, 0:s]
        m = jnp.max(logits, axis=-1, keepdims=True)
        p = jnp.exp(logits - m)
        l = jnp.sum(p, axis=-1, keepdims=True)
        o_ref[:, pl.ds(lo, HEAD_DIM)] = _dot(p.astype(BF16), v_ref[0:s, :]) / l
        return carry

    lax.fori_loop(0, ATT_HEADS, hbody, 0)


def _dsa_kernel(slopes_ref, q_ref, k_ref, v_ref, qi_ref, ki_ref, wi_ref, o_ref, key_ref, d_ref, *, seq, topk):
    j = pl.program_id(1)
    n_tiles = (Q_BLOCK * (j + 1) + KEY_TILE - 1) // KEY_TILE
    for n in range(1, seq // KEY_TILE + 1):
        @pl.when(n_tiles == n)
        def _(n=n):
            _dsa_block(n * KEY_TILE, j, q_ref, k_ref, v_ref, qi_ref, ki_ref, wi_ref, slopes_ref, o_ref,
                       key_ref, d_ref, topk=topk)


def _dsa(q, k, v, qi, ki, wi):
    nb, seq, _ = q.shape
    topk = min(TOPK_MAX, seq // 4)
    assert seq <= 2048 and seq % KEY_TILE == 0 and KEY_TILE % Q_BLOCK == 0 and CHUNK == 64

    def qtile(w):
        return pl.BlockSpec((None, Q_BLOCK, w), lambda b, j: (b, j, 0))

    def full(w):
        return pl.BlockSpec((None, seq, w), lambda b, j: (b, 0, 0))

    return pl.pallas_call(
        functools.partial(_dsa_kernel, seq=seq, topk=topk),
        grid=(nb, seq // Q_BLOCK),
        in_specs=[pl.BlockSpec(memory_space=pltpu.SMEM),
                  qtile(ATT_WIDTH), full(HEAD_DIM), full(HEAD_DIM),
                  qtile(IDX_HEADS * IDX_DIM), full(IDX_DIM), qtile(IDX_HEADS)],
        out_specs=qtile(ATT_WIDTH),
        out_shape=jax.ShapeDtypeStruct((nb, seq, ATT_WIDTH), F32),
        scratch_shapes=[pltpu.VMEM((Q_BLOCK, seq), jnp.int32), pltpu.VMEM((Q_BLOCK, seq), F32)],
        compiler_params=pltpu.CompilerParams(
            dimension_semantics=("arbitrary", "arbitrary"), vmem_limit_bytes=VMEM_LIMIT),
        name="dsa",
    )(jnp.asarray(SLOPES, F32), q, k, v, qi, ki, wi)


def _b_out_kernel(x_ref, ymix_ref, wqm_ref, wgate_ref, kv_ref, wout_ref, pg_ref, pb_ref, o_ref):
    x = x_ref[...]
    xb = x.astype(BF16)
    y_mem = _mem_attn(_dot(xb, wqm_ref[...]), kv_ref)
    gate = _dot(xb, wgate_ref[...])
    o_ref[...] = _gated_out(x, ymix_ref[...], y_mem, gate, wout_ref, pg_ref, pb_ref)


def _b_out(x, y_mix, layer, wqm, wgate, kv, w_out, pg, pb, *, tt=256):
    nb, seq, _ = x.shape
    return pl.pallas_call(
        _b_out_kernel,
        grid=(nb, seq // tt),
        in_specs=[
            pl.BlockSpec((None, tt, D_MODEL), lambda b, t: (b, t, 0)),
            pl.BlockSpec((None, tt, ATT_WIDTH), lambda b, t: (b, t, 0)),
            _const_spec(wqm.shape),
            _const_spec(wgate.shape),
            pl.BlockSpec((None, None, N_MEM, 2 * MEM_WIDTH), lambda b, t: (layer, b, 0, 0)),
            _const_spec((MIX_WIDTH, D_MODEL)),
            _const_spec((1, D_MODEL)),
            _const_spec((1, D_MODEL)),
        ],
        out_specs=pl.BlockSpec((None, tt, D_MODEL), lambda b, t: (b, t, 0)),
        out_shape=jax.ShapeDtypeStruct(x.shape, F32),
        compiler_params=pltpu.CompilerParams(
            dimension_semantics=("arbitrary", "arbitrary"), vmem_limit_bytes=VMEM_LIMIT),
        name="b_out",
    )(x, y_mix, wqm, wgate, kv, w_out, pg, pb)


def _pad_cols(w, n):
    return jnp.pad(w, ((0, 0), (0, n - w.shape[1])))


def kernel(x, mem, mem_ln_g, mem_ln_b, a_w_in, a_conv_w, a_conv_b, a_ln_g, a_ln_b, a_w_mkv, a_w_out,
           a_post_g, a_post_b, b_w_in, b_w_mkv, b_w_out, b_post_g, b_post_b):
    row = lambda v: v.reshape(1, -1)
    w_mkv_all = jnp.stack([(a_w_mkv if i % 2 == 0 else b_w_mkv)[i // 2] for i in range(DEPTH)]).astype(BF16)
    kv = _memkv(mem, row(mem_ln_g), row(mem_ln_b), w_mkv_all)

    c0 = ATT_WIDTH + 2 * HEAD_DIM
    c1 = c0 + IDX_HEADS * IDX_DIM
    c2 = c1 + IDX_DIM
    c3 = c2 + IDX_HEADS
    c4 = c3 + MEM_WIDTH

    for i in range(DEPTH):
        j = i // 2
        if i % 2 == 0:
            x = _layer_a(
                x, i, a_w_in[j].astype(BF16),
                jnp.pad(a_conv_w[j], ((0, 1), (0, 0))), row(a_conv_b[j]), row(a_ln_g[j]), row(a_ln_b[j]),
                kv, a_w_out[j].astype(BF16), row(a_post_g[j]), row(a_post_b[j]))
        else:
            w = b_w_in[j]
            wqkv = w[:, :c0].astype(BF16)
            widx = jnp.concatenate(
                [w[:, c0:c1], _pad_cols(w[:, c1:c2], LANES), _pad_cols(w[:, c2:c3], LANES)], axis=1).astype(BF16)
            q, k, v, qi, ki, wi = _b_proj(x, wqkv, widx)
            y_mix = _dsa(q, k, v, qi, ki, wi)
            x = _b_out(x, y_mix, i, w[:, c3:c4].astype(BF16), w[:, c4:].astype(BF16), kv,
                       b_w_out[j].astype(BF16), row(b_post_g[j]), row(b_post_b[j]))
    return x
```

```python
import functools
import math

import numpy as np
import jax
import jax.numpy as jnp
from jax import lax
from jax.experimental import pallas as pl
from jax.experimental.pallas import tpu as pltpu

D_MODEL = 1024
DEPTH = 4
CHUNK = 64
Q_BLOCK = 128
HEAD_DIM = 128
N_MEM = 256
MEM_HEADS = 4
MEM_WIDTH = MEM_HEADS * HEAD_DIM
MIX_WIDTH = 2 * D_MODEL
CONV_WIDTH = MIX_WIDTH - MEM_WIDTH
CONV_KERNEL = 31
ATT_HEADS = CONV_WIDTH // HEAD_DIM
ATT_WIDTH = ATT_HEADS * HEAD_DIM
IDX_HEADS = 8
IDX_DIM = 64
TOPK_MAX = 256


def _f32(v):
    return float(np.float32(v))


ALPHA = _f32((2 * DEPTH) ** 0.25)
LN_EPS = _f32(1e-5)
NEG = _f32(-1e30)
SCALE = _f32(HEAD_DIM ** -0.5)

LANES = 128
CONV_HALO = 32
VMEM_LIMIT = 56 * 1024 * 1024

BF16 = jnp.bfloat16
F32 = jnp.float32


def _alibi_slopes(n):
    p = 2 ** int(math.floor(math.log2(n)))
    base = [2.0 ** (-8.0 * (i + 1) / p) for i in range(p)]
    extra = [2.0 ** (-4.0 * (2 * i + 1) / p) for i in range(n - p)]
    return [_f32(s) for s in base + extra]


SLOPES = _alibi_slopes(ATT_HEADS)


def _ln(x, g, b):
    mu = jnp.mean(x, axis=-1, keepdims=True)
    xc = x - mu
    var = jnp.mean(xc * xc, axis=-1, keepdims=True)
    return xc * lax.rsqrt(var + LN_EPS) * g + b


def _sigmoid(x):
    return 1.0 / (1.0 + jnp.exp(-x))


def _dot(a, b):
    return jnp.dot(a, b, preferred_element_type=F32)


def _dot_nt(a, b):
    return lax.dot_general(a, b, (((1,), (1,)), ((), ())), preferred_element_type=F32)


def _const_spec(shape):
    nd = len(shape)
    return pl.BlockSpec(shape, lambda *_: (0,) * nd, pipeline_mode=pl.Buffered(1))


def _memkv_kernel(mem_ref, g_ref, b_ref, w_ref, kv_ref):
    m = _ln(mem_ref[...], g_ref[...], b_ref[...])
    kv_ref[...] = _dot(m.astype(BF16), w_ref[...]).astype(BF16)


def _memkv(mem, g, b, w_all):
    nb = mem.shape[0]
    return pl.pallas_call(
        _memkv_kernel,
        grid=(DEPTH, nb),
        in_specs=[
            pl.BlockSpec((None, N_MEM, D_MODEL), lambda l, i: (i, 0, 0)),
            pl.BlockSpec((1, D_MODEL), lambda l, i: (0, 0)),
            pl.BlockSpec((1, D_MODEL), lambda l, i: (0, 0)),
            pl.BlockSpec((None, D_MODEL, 2 * MEM_WIDTH), lambda l, i: (l, 0, 0)),
        ],
        out_specs=pl.BlockSpec((None, None, N_MEM, 2 * MEM_WIDTH), lambda l, i: (l, i, 0, 0)),
        out_shape=jax.ShapeDtypeStruct((DEPTH, nb, N_MEM, 2 * MEM_WIDTH), BF16),
        compiler_params=pltpu.CompilerParams(dimension_semantics=("arbitrary", "arbitrary")),
        name="memkv",
    )(mem, g, b, w_all)


def _mem_attn(qm, kv_ref):
    outs = []
    for h in range(MEM_HEADS):
        lo = h * HEAD_DIM
        qh = (qm[:, lo:lo + HEAD_DIM] * SCALE).astype(BF16)
        kh = kv_ref[:, lo:lo + HEAD_DIM]
        vh = kv_ref[:, MEM_WIDTH + lo:MEM_WIDTH + lo + HEAD_DIM]
        s = _dot_nt(qh, kh)
        m = jnp.max(s, axis=-1, keepdims=True)
        p = jnp.exp(s - m)
        l = jnp.sum(p, axis=-1, keepdims=True)
        outs.append(_dot(p.astype(BF16), vh) / l)
    return outs


def _gated_out(x, y_mix, y_mem_heads, gate, wout_ref, pg_ref, pb_ref):
    sg = gate * _sigmoid(gate)
    acc = _dot((y_mix * sg[:, :CONV_WIDTH]).astype(BF16), wout_ref[0:CONV_WIDTH, :])
    for h, yh in enumerate(y_mem_heads):
        lo = CONV_WIDTH + h * HEAD_DIM
        acc = acc + _dot((yh * sg[:, lo:lo + HEAD_DIM]).astype(BF16), wout_ref[lo:lo + HEAD_DIM, :])
    return _ln(ALPHA * x + acc, pg_ref[...], pb_ref[...])


GLU_CHUNK = 2 * LANES


def _layer_a_kernel(x_ref, wag_ref, wgq_ref, cw_ref, cb_ref, lg_ref, lb_ref, kv_ref, wout_ref, pg_ref, pb_ref,
                    o_ref, xbuf, hbuf, cbuf, gqbuf, *, tt):
    t = pl.program_id(1)
    x = x_ref[...]
    xbuf[...] = x.astype(BF16)

    @pl.when(t == 0)
    def _():
        hbuf[0:CONV_HALO, :] = jnp.zeros((CONV_HALO, CONV_WIDTH), F32)

    @pl.when(t > 0)
    def _():
        hbuf[0:CONV_HALO, :] = hbuf[tt:tt + CONV_HALO, :]

    n_chunks = CONV_WIDTH // LANES
    n_gq = (MEM_WIDTH + MIX_WIDTH) // GLU_CHUNK
    off = CONV_HALO - (CONV_KERNEL - 1)
    half = tt // 2

    def glu_chunk(c):
        ag = _dot(xbuf[...], wag_ref[:, c * GLU_CHUNK:(c + 1) * GLU_CHUNK])
        hbuf[CONV_HALO:CONV_HALO + tt, c * LANES:(c + 1) * LANES] = ag[:, :LANES] * _sigmoid(ag[:, LANES:])

    def conv_chunk(c):
        lanes = slice(c * LANES, (c + 1) * LANES)
        rows = half + CONV_HALO
        for r0 in range(0, tt, half):
            strip = hbuf[r0:r0 + rows, lanes]
            acc = jnp.broadcast_to(cb_ref[:, lanes], (half, LANES))
            for phase in range(8):
                win = pltpu.roll(strip, rows - phase, axis=0) if phase else strip
                for a in range(CONV_HALO // 8 + 1):
                    k = 8 * a + phase - off
                    if 0 <= k < CONV_KERNEL:
                        acc = acc + win[8 * a:8 * a + half] * cw_ref[k:k + 1, lanes]
            cbuf[r0:r0 + half, lanes] = acc

    glu_chunk(0)
    for c in range(n_chunks):
        if c + 1 < n_chunks:
            glu_chunk(c + 1)
        if c < n_gq:
            gqbuf[:, c * GLU_CHUNK:(c + 1) * GLU_CHUNK] = _dot(xbuf[...], wgq_ref[:, c * GLU_CHUNK:(c + 1) * GLU_CHUNK])
        conv_chunk(c)

    yn = _ln(cbuf[...], lg_ref[...], lb_ref[...])
    y_mix = yn * _sigmoid(yn)
    y_mem = _mem_attn(gqbuf[:, 0:MEM_WIDTH], kv_ref)
    o_ref[...] = _gated_out(x, y_mix, y_mem, gqbuf[:, MEM_WIDTH:], wout_ref, pg_ref, pb_ref)


def _layer_a(x, layer, w_ag, w_gq, conv_w, conv_b, ln_g, ln_b, kv, w_out, pg, pb, *, tt=256):
    nb, seq, _ = x.shape
    assert (MEM_WIDTH + MIX_WIDTH) % GLU_CHUNK == 0 and (MEM_WIDTH + MIX_WIDTH) // GLU_CHUNK <= CONV_WIDTH // LANES
    return pl.pallas_call(
        functools.partial(_layer_a_kernel, tt=tt),
        grid=(nb, seq // tt),
        in_specs=[
            pl.BlockSpec((None, tt, D_MODEL), lambda b, t: (b, t, 0)),
            _const_spec(w_ag.shape),
            _const_spec(w_gq.shape),
            _const_spec((CONV_KERNEL + 1, CONV_WIDTH)),
            _const_spec((1, CONV_WIDTH)),
            _const_spec((1, CONV_WIDTH)),
            _const_spec((1, CONV_WIDTH)),
            pl.BlockSpec((None, None, N_MEM, 2 * MEM_WIDTH), lambda b, t: (layer, b, 0, 0)),
            _const_spec((MIX_WIDTH, D_MODEL)),
            _const_spec((1, D_MODEL)),
            _const_spec((1, D_MODEL)),
        ],
        out_specs=pl.BlockSpec((None, tt, D_MODEL), lambda b, t: (b, t, 0)),
        out_shape=jax.ShapeDtypeStruct(x.shape, F32),
        scratch_shapes=[
            pltpu.VMEM((tt, D_MODEL), BF16),
            pltpu.VMEM((tt + CONV_HALO, CONV_WIDTH), F32),
            pltpu.VMEM((tt, CONV_WIDTH), F32),
            pltpu.VMEM((tt, MEM_WIDTH + MIX_WIDTH), F32),
        ],
        compiler_params=pltpu.CompilerParams(
            dimension_semantics=("arbitrary", "arbitrary"), vmem_limit_bytes=VMEM_LIMIT),
        name="layer_a",
    )(x, w_ag, w_gq, conv_w, conv_b, ln_g, ln_b, kv, w_out, pg, pb)


IDX_PAD = IDX_HEADS * IDX_DIM + 2 * LANES


def _b_proj_kernel(x_ref, wqkv_ref, widx_ref, q_ref, k_ref, v_ref, qi_ref, ki_ref, wi_ref):
    xb = x_ref[...].astype(BF16)
    qkv = _dot(xb, wqkv_ref[...])
    for h in range(ATT_HEADS):
        q_ref[h] = (qkv[:, h * HEAD_DIM:(h + 1) * HEAD_DIM] * SCALE).astype(BF16)
    k_ref[...] = qkv[:, ATT_WIDTH:ATT_WIDTH + HEAD_DIM].astype(BF16)
    v_ref[...] = qkv[:, ATT_WIDTH + HEAD_DIM:].astype(BF16)
    idx = _dot(xb, widx_ref[...])
    nq = IDX_HEADS * IDX_DIM
    qi_ref[...] = idx[:, :nq].astype(BF16)
    ki_ref[...] = idx[:, nq:nq + IDX_DIM].astype(BF16)
    wi_ref[...] = idx[:, nq + LANES:nq + LANES + IDX_HEADS]


def _b_proj(x, wqkv, widx, *, tt=512):
    nb, seq, _ = x.shape
    nq = IDX_HEADS * IDX_DIM

    def tile(w):
        return pl.BlockSpec((None, tt, w), lambda b, t: (b, t, 0))

    return pl.pallas_call(
        _b_proj_kernel,
        grid=(nb, seq // tt),
        in_specs=[tile(D_MODEL), _const_spec(wqkv.shape), _const_spec(widx.shape)],
        out_specs=[pl.BlockSpec((None, ATT_HEADS, tt, HEAD_DIM), lambda b, t: (b, 0, t, 0)),
                   tile(HEAD_DIM), tile(HEAD_DIM), tile(nq), tile(IDX_DIM), tile(IDX_HEADS)],
        out_shape=[
            jax.ShapeDtypeStruct((nb, ATT_HEADS, seq, HEAD_DIM), BF16),
            jax.ShapeDtypeStruct((nb, seq, HEAD_DIM), BF16),
            jax.ShapeDtypeStruct((nb, seq, HEAD_DIM), BF16),
            jax.ShapeDtypeStruct((nb, seq, nq), BF16),
            jax.ShapeDtypeStruct((nb, seq, IDX_DIM), BF16),
            jax.ShapeDtypeStruct((nb, seq, IDX_HEADS), F32),
        ],
        compiler_params=pltpu.CompilerParams(
            dimension_semantics=("arbitrary", "arbitrary"), vmem_limit_bytes=VMEM_LIMIT),
        name="b_proj",
    )(x, wqkv, widx)


NEG_KEY = int(np.float32(NEG).view(np.int32)) ^ 0x7FFFFFFF
MASKED_DIST = _f32(-NEG / min(SLOPES))
Q_TILE = 256
INT32_MIN = int(jnp.iinfo(jnp.int32).min)


def _count(pred):
    return jnp.sum(jnp.where(pred, 1.0, 0.0), axis=-1, keepdims=True)


def _dsa_block(s, jj, slopes_ref, q_ref, k_ref, v_ref, qi_ref, ki_ref, wi_ref, o_ref,
               key_ref, d_ref, lg_ref, p_ref, l_ref, *, topk):
    qt = Q_TILE
    row = lax.broadcasted_iota(jnp.int32, (qt, s), 0)
    col = lax.broadcasted_iota(jnp.int32, (qt, s), 1)
    qpos = row + jj * qt
    admiss = (col >> 6) <= (qpos >> 6)

    ki = ki_ref[0:s, :]
    wi = wi_ref[...]
    isc = jnp.zeros((qt, s), F32)
    for h in range(IDX_HEADS):
        sc = _dot_nt(qi_ref[:, h * IDX_DIM:(h + 1) * IDX_DIM], ki)
        isc = isc + wi[:, h:h + 1] * jnp.maximum(sc, 0.0)
    isc = jnp.where(admiss, isc, NEG) + 0.0

    bits = pltpu.bitcast(isc, jnp.int32)
    key_ref[:, 0:s] = jnp.where(bits < 0, bits ^ jnp.int32(0x7FFFFFFF), bits)

    kf = F32(topk)
    half = qt // 2

    def partial_count(r0, cand):
        cb = jnp.broadcast_to(cand, (half, LANES))
        acc = jnp.zeros((half, LANES), F32)
        for t in range(s // LANES):
            acc = acc + jnp.where(key_ref[r0:r0 + half, t * LANES:(t + 1) * LANES] >= cb, 1.0, 0.0)
        return acc

    def decide(part, cand, lo):
        return jnp.where(jnp.sum(part, axis=-1, keepdims=True) >= kf, cand, lo)

    def tbody(i, carry):
        lo_a, lo_b, cand_b, part_b = carry
        cand_a = lo_a + (jnp.int32(1) << (31 - i))
        part_a = partial_count(0, cand_a)
        lo_b = decide(part_b, cand_b, lo_b)
        lo_a = decide(part_a, cand_a, lo_a)
        cand_b = lo_b + (jnp.int32(1) << jnp.maximum(30 - i, 0))
        return lo_a, lo_b, cand_b, partial_count(half, cand_b)

    lo0 = jnp.full((half, 1), INT32_MIN, jnp.int32)
    cand0 = jnp.zeros((half, 1), jnp.int32)
    lo_a, lo_b, _, _ = lax.fori_loop(0, 32, tbody, (lo0, lo0, cand0, partial_count(half, cand0)))
    thr = jnp.concatenate([lo_a, lo_b], axis=0)

    keys = key_ref[:, 0:s]
    n_ge = _count(keys >= thr)
    dist = jnp.abs(qpos - col).astype(F32)
    has_tie = jnp.max(jnp.where((n_ge > kf) & (thr > NEG_KEY), 1.0, 0.0)) > 0.0

    @pl.when(jnp.logical_not(has_tie))
    def _():
        d_ref[:, 0:s] = jnp.where(admiss, jnp.where(keys >= thr, dist, MASKED_DIST), MASKED_DIST)

    @pl.when(has_tie)
    def _():
        eq = keys == thr
        need = kf - _count(keys > thr)

        def cbody(i, cut):
            cand = cut + (jnp.int32(1) << (11 - i))
            cnt = _count(eq & (col < cand))
            return jnp.where((cnt <= need) & (cand <= s), cand, cut)

        cut = lax.fori_loop(0, 12, cbody, jnp.zeros((qt, 1), jnp.int32))
        sel = (keys > thr) | (eq & (col < cut))
        d_ref[:, 0:s] = jnp.where(admiss, jnp.where(sel, dist, MASKED_DIST), MASKED_DIST)

    def qk(h, slot):
        lg_ref[slot, :, 0:s] = _dot_nt(q_ref[h], k_ref[0:s, :])

    def softmax(h, slot):
        lg = lg_ref[slot, :, 0:s] - slopes_ref[h] * d_ref[:, 0:s]
        p = jnp.exp(lg - jnp.max(lg, axis=-1, keepdims=True))
        p_ref[slot, :, 0:s] = p.astype(BF16)
        l_ref[slot] = jnp.broadcast_to(jnp.sum(p, axis=-1, keepdims=True), (qt, HEAD_DIM))

    def pv(h, slot):
        lo = pl.multiple_of(h * HEAD_DIM, HEAD_DIM)
        o_ref[:, pl.ds(lo, HEAD_DIM)] = _dot(p_ref[slot, :, 0:s], v_ref[0:s, :]) / l_ref[slot]

    qk(0, 0)
    qk(1, 1)
    softmax(0, 0)

    def hbody(t, carry):
        a = 2 * t
        pv(a, 0)
        softmax(a + 1, 1)
        qk(a + 2, 0)
        pv(a + 1, 1)
        softmax(a + 2, 0)
        qk(a + 3, 1)
        return carry

    lax.fori_loop(0, ATT_HEADS // 2 - 1, hbody, 0)
    pv(ATT_HEADS - 2, 0)
    softmax(ATT_HEADS - 1, 1)
    pv(ATT_HEADS - 1, 1)


def _dsa_kernel(slopes_ref, q_ref, k_ref, v_ref, qi_ref, ki_ref, wi_ref, o_ref,
                key_ref, d_ref, lg_ref, p_ref, l_ref, *, seq, topk):
    jj = pl.program_id(1)
    for n in range(1, seq // Q_TILE + 1):
        @pl.when(jj == n - 1)
        def _(n=n):
            _dsa_block(n * Q_TILE, jj, slopes_ref, q_ref, k_ref, v_ref, qi_ref, ki_ref, wi_ref, o_ref,
                       key_ref, d_ref, lg_ref, p_ref, l_ref, topk=topk)


def _dsa(q, k, v, qi, ki, wi):
    nb, _, seq, _ = q.shape
    topk = min(TOPK_MAX, seq // 4)
    assert seq <= 2048 and seq % Q_TILE == 0 and Q_TILE % Q_BLOCK == 0 and CHUNK == 64
    assert ATT_HEADS % 2 == 0

    def qtile(w):
        return pl.BlockSpec((None, Q_TILE, w), lambda b, j: (b, j, 0))

    def full(w):
        return pl.BlockSpec((None, seq, w), lambda b, j: (b, 0, 0))

    return pl.pallas_call(
        functools.partial(_dsa_kernel, seq=seq, topk=topk),
        grid=(nb, seq // Q_TILE),
        in_specs=[pl.BlockSpec(memory_space=pltpu.SMEM),
                  pl.BlockSpec((None, ATT_HEADS, Q_TILE, HEAD_DIM), lambda b, j: (b, 0, j, 0)),
                  full(HEAD_DIM), full(HEAD_DIM),
                  qtile(IDX_HEADS * IDX_DIM), full(IDX_DIM), qtile(IDX_HEADS)],
        out_specs=qtile(ATT_WIDTH),
        out_shape=jax.ShapeDtypeStruct((nb, seq, ATT_WIDTH), F32),
        scratch_shapes=[pltpu.VMEM((Q_TILE, seq), jnp.int32), pltpu.VMEM((Q_TILE, seq), F32),
                        pltpu.VMEM((2, Q_TILE, seq), F32), pltpu.VMEM((2, Q_TILE, seq), BF16),
                        pltpu.VMEM((2, Q_TILE, HEAD_DIM), F32)],
        compiler_params=pltpu.CompilerParams(
            dimension_semantics=("arbitrary", "arbitrary"), vmem_limit_bytes=VMEM_LIMIT),
        name="dsa",
    )(jnp.asarray(SLOPES, F32), q, k, v, qi, ki, wi)


def _b_out_kernel(x_ref, ymix_ref, wqm_ref, wgate_ref, kv_ref, wout_ref, pg_ref, pb_ref, o_ref):
    x = x_ref[...]
    xb = x.astype(BF16)
    y_mem = _mem_attn(_dot(xb, wqm_ref[...]), kv_ref)
    gate = _dot(xb, wgate_ref[...])
    o_ref[...] = _gated_out(x, ymix_ref[...], y_mem, gate, wout_ref, pg_ref, pb_ref)


def _b_out(x, y_mix, layer, wqm, wgate, kv, w_out, pg, pb, *, tt=256):
    nb, seq, _ = x.shape
    return pl.pallas_call(
        _b_out_kernel,
        grid=(nb, seq // tt),
        in_specs=[
            pl.BlockSpec((None, tt, D_MODEL), lambda b, t: (b, t, 0)),
            pl.BlockSpec((None, tt, ATT_WIDTH), lambda b, t: (b, t, 0)),
            _const_spec(wqm.shape),
            _const_spec(wgate.shape),
            pl.BlockSpec((None, None, N_MEM, 2 * MEM_WIDTH), lambda b, t: (layer, b, 0, 0)),
            _const_spec((MIX_WIDTH, D_MODEL)),
            _const_spec((1, D_MODEL)),
            _const_spec((1, D_MODEL)),
        ],
        out_specs=pl.BlockSpec((None, tt, D_MODEL), lambda b, t: (b, t, 0)),
        out_shape=jax.ShapeDtypeStruct(x.shape, F32),
        compiler_params=pltpu.CompilerParams(
            dimension_semantics=("arbitrary", "arbitrary"), vmem_limit_bytes=VMEM_LIMIT),
        name="b_out",
    )(x, y_mix, wqm, wgate, kv, w_out, pg, pb)


def _pad_cols(w, n):
    return jnp.pad(w, ((0, 0), (0, n - w.shape[1])))


def kernel(x, mem, mem_ln_g, mem_ln_b, a_w_in, a_conv_w, a_conv_b, a_ln_g, a_ln_b, a_w_mkv, a_w_out,
           a_post_g, a_post_b, b_w_in, b_w_mkv, b_w_out, b_post_g, b_post_b):
    row = lambda v: v.reshape(1, -1)
    w_mkv_all = jnp.stack([(a_w_mkv if i % 2 == 0 else b_w_mkv)[i // 2] for i in range(DEPTH)]).astype(BF16)
    kv = _memkv(mem, row(mem_ln_g), row(mem_ln_b), w_mkv_all)

    c0 = ATT_WIDTH + 2 * HEAD_DIM
    c1 = c0 + IDX_HEADS * IDX_DIM
    c2 = c1 + IDX_DIM
    c3 = c2 + IDX_HEADS
    c4 = c3 + MEM_WIDTH

    for i in range(DEPTH):
        j = i // 2
        if i % 2 == 0:
            w = a_w_in[j]
            nc = CONV_WIDTH // LANES
            w_ag = jnp.stack([w[:, :CONV_WIDTH].reshape(D_MODEL, nc, LANES),
                              w[:, CONV_WIDTH:2 * CONV_WIDTH].reshape(D_MODEL, nc, LANES)], axis=2)
            x = _layer_a(
                x, i, w_ag.reshape(D_MODEL, 2 * CONV_WIDTH).astype(BF16), w[:, 2 * CONV_WIDTH:].astype(BF16),
                jnp.pad(a_conv_w[j], ((0, 1), (0, 0))), row(a_conv_b[j]), row(a_ln_g[j]), row(a_ln_b[j]),
                kv, a_w_out[j].astype(BF16), row(a_post_g[j]), row(a_post_b[j]))
        else:
            w = b_w_in[j]
            wqkv = w[:, :c0].astype(BF16)
            widx = jnp.concatenate(
                [w[:, c0:c1], _pad_cols(w[:, c1:c2], LANES), _pad_cols(w[:, c2:c3], LANES)], axis=1).astype(BF16)
            q, k, v, qi, ki, wi = _b_proj(x, wqkv, widx)
            y_mix = _dsa(q, k, v, qi, ki, wi)
            x = _b_out(x, y_mix, i, w[:, c3:c4].astype(BF16), w[:, c4:].astype(BF16), kv,
                       b_w_out[j].astype(BF16), row(b_post_g[j]), row(b_post_b[j]))
    return x
```

```python
import functools
import math

import numpy as np
import jax
import jax.numpy as jnp
from jax import lax
from jax.experimental import pallas as pl
from jax.experimental.pallas import tpu as pltpu

D_MODEL = 1024
DEPTH = 4
CHUNK = 64
Q_BLOCK = 128
HEAD_DIM = 128
N_MEM = 256
MEM_HEADS = 4
MEM_WIDTH = MEM_HEADS * HEAD_DIM
MIX_WIDTH = 2 * D_MODEL
CONV_WIDTH = MIX_WIDTH - MEM_WIDTH
CONV_KERNEL = 31
ATT_HEADS = CONV_WIDTH // HEAD_DIM
ATT_WIDTH = ATT_HEADS * HEAD_DIM
IDX_HEADS = 8
IDX_DIM = 64
TOPK_MAX = 256


def _f32(v):
    return float(np.float32(v))


ALPHA = _f32((2 * DEPTH) ** 0.25)
LN_EPS = _f32(1e-5)
NEG = _f32(-1e30)
SCALE = _f32(HEAD_DIM ** -0.5)

LANES = 128
CONV_HALO = 32
VMEM_LIMIT = 56 * 1024 * 1024

BF16 = jnp.bfloat16
F32 = jnp.float32


def _alibi_slopes(n):
    p = 2 ** int(math.floor(math.log2(n)))
    base = [2.0 ** (-8.0 * (i + 1) / p) for i in range(p)]
    extra = [2.0 ** (-4.0 * (2 * i + 1) / p) for i in range(n - p)]
    return [_f32(s) for s in base + extra]


SLOPES = _alibi_slopes(ATT_HEADS)


def _ln(x, g, b):
    mu = jnp.mean(x, axis=-1, keepdims=True)
    xc = x - mu
    var = jnp.mean(xc * xc, axis=-1, keepdims=True)
    return xc * lax.rsqrt(var + LN_EPS) * g + b


def _sigmoid(x):
    return 1.0 / (1.0 + jnp.exp(-x))


def _dot(a, b):
    return jnp.dot(a, b, preferred_element_type=F32)


def _dot_nt(a, b):
    return lax.dot_general(a, b, (((1,), (1,)), ((), ())), preferred_element_type=F32)


def _const_spec(shape):
    nd = len(shape)
    return pl.BlockSpec(shape, lambda *_: (0,) * nd, pipeline_mode=pl.Buffered(1))


def _memkv_kernel(mem_ref, g_ref, b_ref, w_ref, kv_ref):
    m = _ln(mem_ref[...], g_ref[...], b_ref[...])
    kv_ref[...] = _dot(m.astype(BF16), w_ref[...]).astype(BF16)


def _memkv(mem, g, b, w_all):
    nb = mem.shape[0]
    return pl.pallas_call(
        _memkv_kernel,
        grid=(DEPTH, nb),
        in_specs=[
            pl.BlockSpec((None, N_MEM, D_MODEL), lambda l, i: (i, 0, 0)),
            pl.BlockSpec((1, D_MODEL), lambda l, i: (0, 0)),
            pl.BlockSpec((1, D_MODEL), lambda l, i: (0, 0)),
            pl.BlockSpec((None, D_MODEL, 2 * MEM_WIDTH), lambda l, i: (l, 0, 0)),
        ],
        out_specs=pl.BlockSpec((None, None, N_MEM, 2 * MEM_WIDTH), lambda l, i: (l, i, 0, 0)),
        out_shape=jax.ShapeDtypeStruct((DEPTH, nb, N_MEM, 2 * MEM_WIDTH), BF16),
        compiler_params=pltpu.CompilerParams(dimension_semantics=("arbitrary", "arbitrary")),
        name="memkv",
    )(mem, g, b, w_all)


def _mem_attn(qm, kv_ref):
    outs = []
    for h in range(MEM_HEADS):
        lo = h * HEAD_DIM
        qh = (qm[:, lo:lo + HEAD_DIM] * SCALE).astype(BF16)
        kh = kv_ref[:, lo:lo + HEAD_DIM]
        vh = kv_ref[:, MEM_WIDTH + lo:MEM_WIDTH + lo + HEAD_DIM]
        s = _dot_nt(qh, kh)
        m = jnp.max(s, axis=-1, keepdims=True)
        p = jnp.exp(s - m)
        l = jnp.sum(p, axis=-1, keepdims=True)
        outs.append(_dot(p.astype(BF16), vh) / l)
    return outs


def _gated_out(x, y_mix, y_mem_heads, gate, wout_ref, pg_ref, pb_ref):
    sg = gate * _sigmoid(gate)
    acc = _dot((y_mix * sg[:, :CONV_WIDTH]).astype(BF16), wout_ref[0:CONV_WIDTH, :])
    for h, yh in enumerate(y_mem_heads):
        lo = CONV_WIDTH + h * HEAD_DIM
        acc = acc + _dot((yh * sg[:, lo:lo + HEAD_DIM]).astype(BF16), wout_ref[lo:lo + HEAD_DIM, :])
    return _ln(ALPHA * x + acc, pg_ref[...], pb_ref[...])


GLU_CHUNK = 2 * LANES


def _layer_a_kernel(x_ref, wag_ref, wgq_ref, cw_ref, cb_ref, lg_ref, lb_ref, kv_ref, wout_ref, pg_ref, pb_ref,
                    o_ref, xbuf, hbuf, cbuf, gqbuf, *, tt):
    t = pl.program_id(1)
    x = x_ref[...]
    xbuf[...] = x.astype(BF16)

    @pl.when(t == 0)
    def _():
        hbuf[0:CONV_HALO, :] = jnp.zeros((CONV_HALO, CONV_WIDTH), F32)

    @pl.when(t > 0)
    def _():
        hbuf[0:CONV_HALO, :] = hbuf[tt:tt + CONV_HALO, :]

    n_chunks = CONV_WIDTH // LANES
    n_gq = (MEM_WIDTH + MIX_WIDTH) // GLU_CHUNK
    off = CONV_HALO - (CONV_KERNEL - 1)
    half = tt // 2

    def glu_chunk(c):
        ag = _dot(xbuf[...], wag_ref[:, c * GLU_CHUNK:(c + 1) * GLU_CHUNK])
        hbuf[CONV_HALO:CONV_HALO + tt, c * LANES:(c + 1) * LANES] = ag[:, :LANES] * _sigmoid(ag[:, LANES:])

    def conv_chunk(c):
        lanes = slice(c * LANES, (c + 1) * LANES)
        rows = half + CONV_HALO
        for r0 in range(0, tt, half):
            strip = hbuf[r0:r0 + rows, lanes]
            acc = jnp.broadcast_to(cb_ref[:, lanes], (half, LANES))
            for phase in range(8):
                win = pltpu.roll(strip, rows - phase, axis=0) if phase else strip
                for a in range(CONV_HALO // 8 + 1):
                    k = 8 * a + phase - off
                    if 0 <= k < CONV_KERNEL:
                        acc = acc + win[8 * a:8 * a + half] * cw_ref[k:k + 1, lanes]
            cbuf[r0:r0 + half, lanes] = acc

    glu_chunk(0)
    for c in range(n_chunks):
        if c + 1 < n_chunks:
            glu_chunk(c + 1)
        if c < n_gq:
            gqbuf[:, c * GLU_CHUNK:(c + 1) * GLU_CHUNK] = _dot(xbuf[...], wgq_ref[:, c * GLU_CHUNK:(c + 1) * GLU_CHUNK])
        conv_chunk(c)

    yn = _ln(cbuf[...], lg_ref[...], lb_ref[...])
    y_mix = yn * _sigmoid(yn)
    y_mem = _mem_attn(gqbuf[:, 0:MEM_WIDTH], kv_ref)
    o_ref[...] = _gated_out(x, y_mix, y_mem, gqbuf[:, MEM_WIDTH:], wout_ref, pg_ref, pb_ref)


def _layer_a(x, layer, w_ag, w_gq, conv_w, conv_b, ln_g, ln_b, kv, w_out, pg, pb, *, tt=256):
    nb, seq, _ = x.shape
    assert (MEM_WIDTH + MIX_WIDTH) % GLU_CHUNK == 0 and (MEM_WIDTH + MIX_WIDTH) // GLU_CHUNK <= CONV_WIDTH // LANES
    return pl.pallas_call(
        functools.partial(_layer_a_kernel, tt=tt),
        grid=(nb, seq // tt),
        in_specs=[
            pl.BlockSpec((None, tt, D_MODEL), lambda b, t: (b, t, 0)),
            _const_spec(w_ag.shape),
            _const_spec(w_gq.shape),
            _const_spec((CONV_KERNEL + 1, CONV_WIDTH)),
            _const_spec((1, CONV_WIDTH)),
            _const_spec((1, CONV_WIDTH)),
            _const_spec((1, CONV_WIDTH)),
            pl.BlockSpec((None, None, N_MEM, 2 * MEM_WIDTH), lambda b, t: (layer, b, 0, 0)),
            _const_spec((MIX_WIDTH, D_MODEL)),
            _const_spec((1, D_MODEL)),
            _const_spec((1, D_MODEL)),
        ],
        out_specs=pl.BlockSpec((None, tt, D_MODEL), lambda b, t: (b, t, 0)),
        out_shape=jax.ShapeDtypeStruct(x.shape, F32),
        scratch_shapes=[
            pltpu.VMEM((tt, D_MODEL), BF16),
            pltpu.VMEM((tt + CONV_HALO, CONV_WIDTH), F32),
            pltpu.VMEM((tt, CONV_WIDTH), F32),
            pltpu.VMEM((tt, MEM_WIDTH + MIX_WIDTH), F32),
        ],
        compiler_params=pltpu.CompilerParams(
            dimension_semantics=("arbitrary", "arbitrary"), vmem_limit_bytes=VMEM_LIMIT),
        name="layer_a",
    )(x, w_ag, w_gq, conv_w, conv_b, ln_g, ln_b, kv, w_out, pg, pb)


IDX_PAD = IDX_HEADS * IDX_DIM + 2 * LANES


def _b_proj_kernel(x_ref, wqkv_ref, widx_ref, q_ref, k_ref, v_ref, qi_ref, ki_ref, wi_ref):
    xb = x_ref[...].astype(BF16)
    qkv = _dot(xb, wqkv_ref[...])
    for h in range(ATT_HEADS):
        q_ref[h] = (qkv[:, h * HEAD_DIM:(h + 1) * HEAD_DIM] * SCALE).astype(BF16)
    k_ref[...] = qkv[:, ATT_WIDTH:ATT_WIDTH + HEAD_DIM].astype(BF16)
    v_ref[...] = qkv[:, ATT_WIDTH + HEAD_DIM:].astype(BF16)
    idx = _dot(xb, widx_ref[...])
    nq = IDX_HEADS * IDX_DIM
    qi_ref[...] = idx[:, :nq].astype(BF16)
    ki_ref[...] = idx[:, nq:nq + IDX_DIM].astype(BF16)
    wi_ref[...] = idx[:, nq + LANES:nq + LANES + IDX_HEADS]


def _b_proj(x, wqkv, widx, *, tt=512):
    nb, seq, _ = x.shape
    nq = IDX_HEADS * IDX_DIM

    def tile(w):
        return pl.BlockSpec((None, tt, w), lambda b, t: (b, t, 0))

    return pl.pallas_call(
        _b_proj_kernel,
        grid=(nb, seq // tt),
        in_specs=[tile(D_MODEL), _const_spec(wqkv.shape), _const_spec(widx.shape)],
        out_specs=[pl.BlockSpec((None, ATT_HEADS, tt, HEAD_DIM), lambda b, t: (b, 0, t, 0)),
                   tile(HEAD_DIM), tile(HEAD_DIM), tile(nq), tile(IDX_DIM), tile(IDX_HEADS)],
        out_shape=[
            jax.ShapeDtypeStruct((nb, ATT_HEADS, seq, HEAD_DIM), BF16),
            jax.ShapeDtypeStruct((nb, seq, HEAD_DIM), BF16),
            jax.ShapeDtypeStruct((nb, seq, HEAD_DIM), BF16),
            jax.ShapeDtypeStruct((nb, seq, nq), BF16),
            jax.ShapeDtypeStruct((nb, seq, IDX_DIM), BF16),
            jax.ShapeDtypeStruct((nb, seq, IDX_HEADS), F32),
        ],
        compiler_params=pltpu.CompilerParams(
            dimension_semantics=("arbitrary", "arbitrary"), vmem_limit_bytes=VMEM_LIMIT),
        name="b_proj",
    )(x, wqkv, widx)


NEG_KEY = int(np.float32(NEG).view(np.int32)) ^ 0x7FFFFFFF
MASKED_DIST = _f32(-NEG / min(SLOPES))
Q_TILE = 256
KEY_STEP = 512
INT32_MIN = int(jnp.iinfo(jnp.int32).min)


def _count(pred):
    return jnp.sum(jnp.where(pred, 1.0, 0.0), axis=-1, keepdims=True)


def _dsa_block(s, jj, slopes_ref, q_ref, k_ref, v_ref, qi_ref, ki_ref, wi_ref, o_ref,
               key_ref, d_ref, lg_ref, p_ref, l_ref, cut_ref, *, topk):
    qt = Q_TILE
    row = lax.broadcasted_iota(jnp.int32, (qt, s), 0)
    col = lax.broadcasted_iota(jnp.int32, (qt, s), 1)
    qpos = row + jj * qt
    admiss = (col >> 6) <= (qpos >> 6)

    ki = ki_ref[0:s, :]
    wi = wi_ref[...]
    isc = jnp.zeros((qt, s), F32)
    for h in range(IDX_HEADS):
        sc = _dot_nt(qi_ref[:, h * IDX_DIM:(h + 1) * IDX_DIM], ki)
        isc = isc + wi[:, h:h + 1] * jnp.maximum(sc, 0.0)
    isc = jnp.where(admiss, isc, NEG) + 0.0

    bits = pltpu.bitcast(isc, jnp.int32)
    key_ref[:, 0:s] = jnp.where(bits < 0, bits ^ jnp.int32(0x7FFFFFFF), bits)

    kf = F32(topk)
    half = qt // 2

    def partial_count(r0, cand):
        cb = jnp.broadcast_to(cand, (half, LANES))
        acc = jnp.zeros((half, LANES), F32)
        for t in range(s // LANES):
            acc = acc + jnp.where(key_ref[r0:r0 + half, t * LANES:(t + 1) * LANES] >= cb, 1.0, 0.0)
        return acc

    def decide(part, cand, lo):
        return jnp.where(jnp.sum(part, axis=-1, keepdims=True) >= kf, cand, lo)

    def tbody(i, carry):
        lo_a, lo_b, cand_b, part_b = carry
        cand_a = lo_a + (jnp.int32(1) << (31 - i))
        part_a = partial_count(0, cand_a)
        lo_b = decide(part_b, cand_b, lo_b)
        lo_a = decide(part_a, cand_a, lo_a)
        cand_b = lo_b + (jnp.int32(1) << jnp.maximum(30 - i, 0))
        return lo_a, lo_b, cand_b, partial_count(half, cand_b)

    lo0 = jnp.full((half, 1), INT32_MIN, jnp.int32)
    cand0 = jnp.zeros((half, 1), jnp.int32)
    lo_a, lo_b, _, _ = lax.fori_loop(0, 32, tbody, (lo0, lo0, cand0, partial_count(half, cand0)))
    thr = jnp.concatenate([lo_a, lo_b], axis=0)

    keys = key_ref[:, 0:s]
    n_ge = _count(keys >= thr)
    dist = jnp.abs(qpos - col).astype(F32)
    has_tie = jnp.max(jnp.where((n_ge > kf) & (thr > NEG_KEY), 1.0, 0.0)) > 0.0

    cut_ref[...] = jnp.full((qt, 1), s, jnp.int32)

    @pl.when(has_tie)
    def _():
        eq = keys == thr
        need = kf - _count(keys > thr)

        def cbody(i, cut):
            cand = cut + (jnp.int32(1) << (11 - i))
            cnt = _count(eq & (col < cand))
            return jnp.where((cnt <= need) & (cand <= s), cand, cut)

        cut_ref[...] = lax.fori_loop(0, 12, cbody, jnp.zeros((qt, 1), jnp.int32))

    tied = jnp.where(col < cut_ref[...], dist, MASKED_DIST)
    d = jnp.where(keys > thr, dist, jnp.where(keys == thr, tied, MASKED_DIST))
    d_ref[:, 0:s] = jnp.where(admiss, d, MASKED_DIST)

    def qk(h, slot):
        lg_ref[slot, :, 0:s] = _dot_nt(q_ref[h], k_ref[0:s, :])

    def softmax(h, slot):
        lg = lg_ref[slot, :, 0:s] - slopes_ref[h] * d_ref[:, 0:s]
        p = jnp.exp(lg - jnp.max(lg, axis=-1, keepdims=True))
        p_ref[slot, :, 0:s] = p.astype(BF16)
        l_ref[slot] = jnp.broadcast_to(jnp.sum(p, axis=-1, keepdims=True), (qt, HEAD_DIM))

    def pv(h, slot):
        lo = pl.multiple_of(h * HEAD_DIM, HEAD_DIM)
        o_ref[:, pl.ds(lo, HEAD_DIM)] = _dot(p_ref[slot, :, 0:s], v_ref[0:s, :]) / l_ref[slot]

    qk(0, 0)
    qk(1, 1)
    softmax(0, 0)

    def hbody(t, carry):
        a = 2 * t
        pv(a, 0)
        softmax(a + 1, 1)
        qk(a + 2, 0)
        pv(a + 1, 1)
        softmax(a + 2, 0)
        qk(a + 3, 1)
        return carry

    lax.fori_loop(0, ATT_HEADS // 2 - 1, hbody, 0)
    pv(ATT_HEADS - 2, 0)
    softmax(ATT_HEADS - 1, 1)
    pv(ATT_HEADS - 1, 1)


def _dsa_kernel(slopes_ref, q_ref, k_ref, v_ref, qi_ref, ki_ref, wi_ref, o_ref,
                key_ref, d_ref, lg_ref, p_ref, l_ref, cut_ref, *, seq, topk):
    jj = pl.program_id(1)
    n_steps = (Q_TILE * (jj + 1) + KEY_STEP - 1) // KEY_STEP
    for n in range(1, seq // KEY_STEP + 1):
        @pl.when(n_steps == n)
        def _(n=n):
            _dsa_block(n * KEY_STEP, jj, slopes_ref, q_ref, k_ref, v_ref, qi_ref, ki_ref, wi_ref, o_ref,
                       key_ref, d_ref, lg_ref, p_ref, l_ref, cut_ref, topk=topk)


def _dsa(q, k, v, qi, ki, wi):
    nb, _, seq, _ = q.shape
    topk = min(TOPK_MAX, seq // 4)
    assert seq <= 2048 and seq % KEY_STEP == 0 and KEY_STEP % Q_TILE == 0 and Q_TILE % Q_BLOCK == 0 and CHUNK == 64
    assert ATT_HEADS % 2 == 0

    def qtile(w):
        return pl.BlockSpec((None, Q_TILE, w), lambda b, j: (b, j, 0))

    def full(w):
        return pl.BlockSpec((None, seq, w), lambda b, j: (b, 0, 0))

    return pl.pallas_call(
        functools.partial(_dsa_kernel, seq=seq, topk=topk),
        grid=(nb, seq // Q_TILE),
        in_specs=[pl.BlockSpec(memory_space=pltpu.SMEM),
                  pl.BlockSpec((None, ATT_HEADS, Q_TILE, HEAD_DIM), lambda b, j: (b, 0, j, 0)),
                  full(HEAD_DIM), full(HEAD_DIM),
                  qtile(IDX_HEADS * IDX_DIM), full(IDX_DIM), qtile(IDX_HEADS)],
        out_specs=qtile(ATT_WIDTH),
        out_shape=jax.ShapeDtypeStruct((nb, seq, ATT_WIDTH), F32),
        scratch_shapes=[pltpu.VMEM((Q_TILE, seq), jnp.int32), pltpu.VMEM((Q_TILE, seq), F32),
                        pltpu.VMEM((2, Q_TILE, seq), F32), pltpu.VMEM((2, Q_TILE, seq), BF16),
                        pltpu.VMEM((2, Q_TILE, HEAD_DIM), F32), pltpu.VMEM((Q_TILE, 1), jnp.int32)],
        compiler_params=pltpu.CompilerParams(
            dimension_semantics=("arbitrary", "arbitrary"), vmem_limit_bytes=VMEM_LIMIT),
        name="dsa",
    )(jnp.asarray(SLOPES, F32), q, k, v, qi, ki, wi)


def _b_out_kernel(x_ref, ymix_ref, wqm_ref, wgate_ref, kv_ref, wout_ref, pg_ref, pb_ref, o_ref):
    x = x_ref[...]
    xb = x.astype(BF16)
    y_mem = _mem_attn(_dot(xb, wqm_ref[...]), kv_ref)
    gate = _dot(xb, wgate_ref[...])
    o_ref[...] = _gated_out(x, ymix_ref[...], y_mem, gate, wout_ref, pg_ref, pb_ref)


def _b_out(x, y_mix, layer, wqm, wgate, kv, w_out, pg, pb, *, tt=256):
    nb, seq, _ = x.shape
    return pl.pallas_call(
        _b_out_kernel,
        grid=(nb, seq // tt),
        in_specs=[
            pl.BlockSpec((None, tt, D_MODEL), lambda b, t: (b, t, 0)),
            pl.BlockSpec((None, tt, ATT_WIDTH), lambda b, t: (b, t, 0)),
            _const_spec(wqm.shape),
            _const_spec(wgate.shape),
            pl.BlockSpec((None, None, N_MEM, 2 * MEM_WIDTH), lambda b, t: (layer, b, 0, 0)),
            _const_spec((MIX_WIDTH, D_MODEL)),
            _const_spec((1, D_MODEL)),
            _const_spec((1, D_MODEL)),
        ],
        out_specs=pl.BlockSpec((None, tt, D_MODEL), lambda b, t: (b, t, 0)),
        out_shape=jax.ShapeDtypeStruct(x.shape, F32),
        compiler_params=pltpu.CompilerParams(
            dimension_semantics=("arbitrary", "arbitrary"), vmem_limit_bytes=VMEM_LIMIT),
        name="b_out",
    )(x, y_mix, wqm, wgate, kv, w_out, pg, pb)


def _pad_cols(w, n):
    return jnp.pad(w, ((0, 0), (0, n - w.shape[1])))


def kernel(x, mem, mem_ln_g, mem_ln_b, a_w_in, a_conv_w, a_conv_b, a_ln_g, a_ln_b, a_w_mkv, a_w_out,
           a_post_g, a_post_b, b_w_in, b_w_mkv, b_w_out, b_post_g, b_post_b):
    row = lambda v: v.reshape(1, -1)
    w_mkv_all = jnp.stack([(a_w_mkv if i % 2 == 0 else b_w_mkv)[i // 2] for i in range(DEPTH)]).astype(BF16)
    kv = _memkv(mem, row(mem_ln_g), row(mem_ln_b), w_mkv_all)

    c0 = ATT_WIDTH + 2 * HEAD_DIM
    c1 = c0 + IDX_HEADS * IDX_DIM
    c2 = c1 + IDX_DIM
    c3 = c2 + IDX_HEADS
    c4 = c3 + MEM_WIDTH

    for i in range(DEPTH):
        j = i // 2
        if i % 2 == 0:
            w = a_w_in[j]
            nc = CONV_WIDTH // LANES
            w_ag = jnp.stack([w[:, :CONV_WIDTH].reshape(D_MODEL, nc, LANES),
                              w[:, CONV_WIDTH:2 * CONV_WIDTH].reshape(D_MODEL, nc, LANES)], axis=2)
            x = _layer_a(
                x, i, w_ag.reshape(D_MODEL, 2 * CONV_WIDTH).astype(BF16), w[:, 2 * CONV_WIDTH:].astype(BF16),
                jnp.pad(a_conv_w[j], ((0, 1), (0, 0))), row(a_conv_b[j]), row(a_ln_g[j]), row(a_ln_b[j]),
                kv, a_w_out[j].astype(BF16), row(a_post_g[j]), row(a_post_b[j]))
        else:
            w = b_w_in[j]
            wqkv = w[:, :c0].astype(BF16)
            widx = jnp.concatenate(
                [w[:, c0:c1], _pad_cols(w[:, c1:c2], LANES), _pad_cols(w[:, c2:c3], LANES)], axis=1).astype(BF16)
            q, k, v, qi, ki, wi = _b_proj(x, wqkv, widx)
            y_mix = _dsa(q, k, v, qi, ki, wi)
            x = _b_out(x, y_mix, i, w[:, c3:c4].astype(BF16), w[:, c4:].astype(BF16), kv,
                       b_w_out[j].astype(BF16), row(b_post_g[j]), row(b_post_b[j]))
    return x
```

```python
import functools
import math

import numpy as np
import jax
import jax.numpy as jnp
from jax import lax
from jax.experimental import pallas as pl
from jax.experimental.pallas import tpu as pltpu

D_MODEL = 1024
DEPTH = 4
CHUNK = 64
Q_BLOCK = 128
HEAD_DIM = 128
N_MEM = 256
MEM_HEADS = 4
MEM_WIDTH = MEM_HEADS * HEAD_DIM
MIX_WIDTH = 2 * D_MODEL
CONV_WIDTH = MIX_WIDTH - MEM_WIDTH
CONV_KERNEL = 31
ATT_HEADS = CONV_WIDTH // HEAD_DIM
ATT_WIDTH = ATT_HEADS * HEAD_DIM
IDX_HEADS = 8
IDX_DIM = 64
TOPK_MAX = 256


def _f32(v):
    return float(np.float32(v))


ALPHA = _f32((2 * DEPTH) ** 0.25)
LN_EPS = _f32(1e-5)
NEG = _f32(-1e30)
SCALE = _f32(HEAD_DIM ** -0.5)
LOG2E = _f32(1.0 / math.log(2.0))

LANES = 128
CONV_HALO = 32
VMEM_LIMIT = 56 * 1024 * 1024

BF16 = jnp.bfloat16
F32 = jnp.float32


def _alibi_slopes(n):
    p = 2 ** int(math.floor(math.log2(n)))
    base = [2.0 ** (-8.0 * (i + 1) / p) for i in range(p)]
    extra = [2.0 ** (-4.0 * (2 * i + 1) / p) for i in range(n - p)]
    return [_f32(s) for s in base + extra]


SLOPES = _alibi_slopes(ATT_HEADS)


def _ln(x, g, b):
    mu = jnp.mean(x, axis=-1, keepdims=True)
    xc = x - mu
    var = jnp.mean(xc * xc, axis=-1, keepdims=True)
    return xc * lax.rsqrt(var + LN_EPS) * g + b


def _sigmoid(x):
    return 1.0 / (1.0 + jnp.exp(-x))


def _dot(a, b):
    return jnp.dot(a, b, preferred_element_type=F32)


def _dot_nt(a, b):
    return lax.dot_general(a, b, (((1,), (1,)), ((), ())), preferred_element_type=F32)


def _const_spec(shape):
    nd = len(shape)
    return pl.BlockSpec(shape, lambda *_: (0,) * nd, pipeline_mode=pl.Buffered(1))


def _memkv_kernel(mem_ref, g_ref, b_ref, w_ref, kv_ref):
    m = _ln(mem_ref[...], g_ref[...], b_ref[...])
    kv_ref[...] = _dot(m.astype(BF16), w_ref[...]).astype(BF16)


def _memkv(mem, g, b, w_all):
    nb = mem.shape[0]
    return pl.pallas_call(
        _memkv_kernel,
        grid=(DEPTH, nb),
        in_specs=[
            pl.BlockSpec((None, N_MEM, D_MODEL), lambda l, i: (i, 0, 0)),
            pl.BlockSpec((1, D_MODEL), lambda l, i: (0, 0)),
            pl.BlockSpec((1, D_MODEL), lambda l, i: (0, 0)),
            pl.BlockSpec((None, D_MODEL, 2 * MEM_WIDTH), lambda l, i: (l, 0, 0)),
        ],
        out_specs=pl.BlockSpec((None, None, N_MEM, 2 * MEM_WIDTH), lambda l, i: (l, i, 0, 0)),
        out_shape=jax.ShapeDtypeStruct((DEPTH, nb, N_MEM, 2 * MEM_WIDTH), BF16),
        compiler_params=pltpu.CompilerParams(dimension_semantics=("arbitrary", "arbitrary")),
        name="memkv",
    )(mem, g, b, w_all)


def _mem_attn(qm, kv_ref):
    outs = []
    for h in range(MEM_HEADS):
        lo = h * HEAD_DIM
        qh = (qm[:, lo:lo + HEAD_DIM] * SCALE).astype(BF16)
        kh = kv_ref[:, lo:lo + HEAD_DIM]
        vh = kv_ref[:, MEM_WIDTH + lo:MEM_WIDTH + lo + HEAD_DIM]
        s = _dot_nt(qh, kh)
        m = jnp.max(s, axis=-1, keepdims=True)
        p = jnp.exp(s - m)
        l = jnp.sum(p, axis=-1, keepdims=True)
        outs.append(_dot(p.astype(BF16), vh) / l)
    return outs


def _gated_out(x, y_mix, y_mem_heads, gate, wout_ref, pg_ref, pb_ref):
    sg = gate * _sigmoid(gate)
    acc = _dot((y_mix * sg[:, :CONV_WIDTH]).astype(BF16), wout_ref[0:CONV_WIDTH, :])
    for h, yh in enumerate(y_mem_heads):
        lo = CONV_WIDTH + h * HEAD_DIM
        acc = acc + _dot((yh * sg[:, lo:lo + HEAD_DIM]).astype(BF16), wout_ref[lo:lo + HEAD_DIM, :])
    return _ln(ALPHA * x + acc, pg_ref[...], pb_ref[...])


GLU_CHUNK = 2 * LANES


def _layer_a_kernel(x_ref, wag_ref, wgq_ref, cw_ref, cb_ref, lg_ref, lb_ref, kv_ref, wout_ref, pg_ref, pb_ref,
                    o_ref, xbuf, hbuf, cbuf, gqbuf, *, tt):
    t = pl.program_id(1)
    x = x_ref[...]
    xbuf[...] = x.astype(BF16)

    @pl.when(t == 0)
    def _():
        hbuf[0:CONV_HALO, :] = jnp.zeros((CONV_HALO, CONV_WIDTH), F32)

    @pl.when(t > 0)
    def _():
        hbuf[0:CONV_HALO, :] = hbuf[tt:tt + CONV_HALO, :]

    n_chunks = CONV_WIDTH // LANES
    n_gq = (MEM_WIDTH + MIX_WIDTH) // GLU_CHUNK
    off = CONV_HALO - (CONV_KERNEL - 1)
    half = tt // 2

    def glu_chunk(c):
        ag = _dot(xbuf[...], wag_ref[:, c * GLU_CHUNK:(c + 1) * GLU_CHUNK])
        hbuf[CONV_HALO:CONV_HALO + tt, c * LANES:(c + 1) * LANES] = ag[:, :LANES] * _sigmoid(ag[:, LANES:])

    def conv_chunk(c):
        lanes = slice(c * LANES, (c + 1) * LANES)
        rows = half + CONV_HALO
        for r0 in range(0, tt, half):
            strip = hbuf[r0:r0 + rows, lanes]
            acc = jnp.broadcast_to(cb_ref[:, lanes], (half, LANES))
            for phase in range(8):
                win = pltpu.roll(strip, rows - phase, axis=0) if phase else strip
                for a in range(CONV_HALO // 8 + 1):
                    k = 8 * a + phase - off
                    if 0 <= k < CONV_KERNEL:
                        acc = acc + win[8 * a:8 * a + half] * cw_ref[k:k + 1, lanes]
            cbuf[r0:r0 + half, lanes] = acc

    glu_chunk(0)
    for c in range(n_chunks):
        if c + 1 < n_chunks:
            glu_chunk(c + 1)
        if c < n_gq:
            gqbuf[:, c * GLU_CHUNK:(c + 1) * GLU_CHUNK] = _dot(xbuf[...], wgq_ref[:, c * GLU_CHUNK:(c + 1) * GLU_CHUNK])
        conv_chunk(c)

    yn = _ln(cbuf[...], lg_ref[...], lb_ref[...])
    y_mix = yn * _sigmoid(yn)
    y_mem = _mem_attn(gqbuf[:, 0:MEM_WIDTH], kv_ref)
    o_ref[...] = _gated_out(x, y_mix, y_mem, gqbuf[:, MEM_WIDTH:], wout_ref, pg_ref, pb_ref)


def _layer_a(x, layer, w_ag, w_gq, conv_w, conv_b, ln_g, ln_b, kv, w_out, pg, pb, *, tt=256):
    nb, seq, _ = x.shape
    assert (MEM_WIDTH + MIX_WIDTH) % GLU_CHUNK == 0 and (MEM_WIDTH + MIX_WIDTH) // GLU_CHUNK <= CONV_WIDTH // LANES
    return pl.pallas_call(
        functools.partial(_layer_a_kernel, tt=tt),
        grid=(nb, seq // tt),
        in_specs=[
            pl.BlockSpec((None, tt, D_MODEL), lambda b, t: (b, t, 0)),
            _const_spec(w_ag.shape),
            _const_spec(w_gq.shape),
            _const_spec((CONV_KERNEL + 1, CONV_WIDTH)),
            _const_spec((1, CONV_WIDTH)),
            _const_spec((1, CONV_WIDTH)),
            _const_spec((1, CONV_WIDTH)),
            pl.BlockSpec((None, None, N_MEM, 2 * MEM_WIDTH), lambda b, t: (layer, b, 0, 0)),
            _const_spec((MIX_WIDTH, D_MODEL)),
            _const_spec((1, D_MODEL)),
            _const_spec((1, D_MODEL)),
        ],
        out_specs=pl.BlockSpec((None, tt, D_MODEL), lambda b, t: (b, t, 0)),
        out_shape=jax.ShapeDtypeStruct(x.shape, F32),
        scratch_shapes=[
            pltpu.VMEM((tt, D_MODEL), BF16),
            pltpu.VMEM((tt + CONV_HALO, CONV_WIDTH), F32),
            pltpu.VMEM((tt, CONV_WIDTH), F32),
            pltpu.VMEM((tt, MEM_WIDTH + MIX_WIDTH), F32),
        ],
        compiler_params=pltpu.CompilerParams(
            dimension_semantics=("arbitrary", "arbitrary"), vmem_limit_bytes=VMEM_LIMIT),
        name="layer_a",
    )(x, w_ag, w_gq, conv_w, conv_b, ln_g, ln_b, kv, w_out, pg, pb)


IDX_PAD = IDX_HEADS * IDX_DIM + 2 * LANES


B_PROJ_COLS = ATT_WIDTH + 2 * HEAD_DIM + IDX_HEADS * IDX_DIM + 2 * LANES
B_OUT_COLS = MEM_WIDTH + MIX_WIDTH


def _half_spec(rows, cols, blk):
    return pl.BlockSpec((rows, cols), lambda *_: (0, blk), pipeline_mode=pl.Buffered(1))


def _b_proj_kernel(x_ref, w_ref, q_ref, k_ref, v_ref, qi_ref, ki_ref, wi_ref):
    xb = x_ref[...].astype(BF16)
    nqkv = ATT_WIDTH + 2 * HEAD_DIM
    qkv = _dot(xb, w_ref[:, 0:nqkv])
    for h in range(ATT_HEADS):
        q_ref[h] = (qkv[:, h * HEAD_DIM:(h + 1) * HEAD_DIM] * (SCALE * LOG2E)).astype(BF16)
    k_ref[...] = qkv[:, ATT_WIDTH:ATT_WIDTH + HEAD_DIM].astype(BF16)
    v_ref[:, 0:HEAD_DIM] = qkv[:, ATT_WIDTH + HEAD_DIM:].astype(BF16)
    v_ref[:, HEAD_DIM:] = jnp.ones((v_ref.shape[0], HEAD_DIM), BF16)
    idx = _dot(xb, w_ref[:, nqkv:])
    nq = IDX_HEADS * IDX_DIM
    qi_ref[...] = idx[:, :nq].astype(BF16)
    ki_ref[...] = idx[:, nq:nq + IDX_DIM].astype(BF16)
    wi_ref[...] = idx[:, nq + LANES:nq + LANES + IDX_HEADS]


def _b_proj(x, w_b, *, tt=512):
    nb, seq, _ = x.shape
    nq = IDX_HEADS * IDX_DIM

    def tile(w):
        return pl.BlockSpec((None, tt, w), lambda b, t: (b, t, 0))

    return pl.pallas_call(
        _b_proj_kernel,
        grid=(nb, seq // tt),
        in_specs=[tile(D_MODEL), _half_spec(D_MODEL, B_PROJ_COLS, 0)],
        out_specs=[pl.BlockSpec((None, ATT_HEADS, tt, HEAD_DIM), lambda b, t: (b, 0, t, 0)),
                   tile(HEAD_DIM), tile(2 * HEAD_DIM), tile(nq), tile(IDX_DIM), tile(IDX_HEADS)],
        out_shape=[
            jax.ShapeDtypeStruct((nb, ATT_HEADS, seq, HEAD_DIM), BF16),
            jax.ShapeDtypeStruct((nb, seq, HEAD_DIM), BF16),
            jax.ShapeDtypeStruct((nb, seq, 2 * HEAD_DIM), BF16),
            jax.ShapeDtypeStruct((nb, seq, nq), BF16),
            jax.ShapeDtypeStruct((nb, seq, IDX_DIM), BF16),
            jax.ShapeDtypeStruct((nb, seq, IDX_HEADS), F32),
        ],
        compiler_params=pltpu.CompilerParams(
            dimension_semantics=("arbitrary", "arbitrary"), vmem_limit_bytes=VMEM_LIMIT),
        name="b_proj",
    )(x, w_b)


NEG_KEY = int(np.float32(NEG).view(np.int32)) ^ 0x7FFFFFFF
MASKED_DIST = _f32(-NEG / min(SLOPES))
Q_TILE = 256
KEY_STEP = 512
INT32_MIN = int(jnp.iinfo(jnp.int32).min)


def _count(pred):
    return jnp.sum(jnp.where(pred, 1.0, 0.0), axis=-1, keepdims=True)


def _dsa_block(s, jj, slopes_ref, q_ref, k_ref, v_ref, qi_ref, ki_ref, wi_ref, o_ref,
               key_ref, d_ref, lg_ref, p_ref, cut_ref, thr_ref, *, topk):
    qt = Q_TILE
    row = lax.broadcasted_iota(jnp.int32, (qt, s), 0)
    col = lax.broadcasted_iota(jnp.int32, (qt, s), 1)
    qpos = row + jj * qt
    admiss = (col >> 6) <= (qpos >> 6)

    ki = ki_ref[0:s, :]
    wi = wi_ref[...]
    isc = jnp.zeros((qt, s), F32)
    for h in range(IDX_HEADS):
        sc = _dot_nt(qi_ref[:, h * IDX_DIM:(h + 1) * IDX_DIM], ki)
        isc = isc + wi[:, h:h + 1] * jnp.maximum(sc, 0.0)
    isc = jnp.where(admiss, isc, NEG) + 0.0

    bits = pltpu.bitcast(isc, jnp.int32)
    key_ref[:, 0:s] = jnp.where(bits < 0, bits ^ jnp.int32(0x7FFFFFFF), bits)

    kf = F32(topk)
    half = qt // 2

    def partial_count(ext, r0, cand):
        cb = jnp.broadcast_to(cand, (half, LANES))
        acc = jnp.zeros((half, LANES), F32)
        for t in range(ext // LANES):
            acc = acc + jnp.where(key_ref[r0:r0 + half, t * LANES:(t + 1) * LANES] >= cb, 1.0, 0.0)
        return acc

    def decide(part, cand, lo):
        return jnp.where(jnp.sum(part, axis=-1, keepdims=True) >= kf, cand, lo)

    def tbody(count, i, carry):
        lo_a, lo_b, cand_b, part_b = carry
        cand_a = lo_a + (jnp.int32(1) << (31 - i))
        part_a = count(0, cand_a)
        lo_b = decide(part_b, cand_b, lo_b)
        lo_a = decide(part_a, cand_a, lo_a)
        cand_b = lo_b + (jnp.int32(1) << jnp.maximum(30 - i, 0))
        return lo_a, lo_b, cand_b, count(half, cand_b)

    lo0 = jnp.full((half, 1), INT32_MIN, jnp.int32)
    cand0 = jnp.zeros((half, 1), jnp.int32)

    for ext in range(s, s - KEY_STEP, -qt):
        @pl.when(qt * (jj + 1) == ext)
        def _(ext=ext):
            count = functools.partial(partial_count, ext)
            lo_a, lo_b, _, _ = lax.fori_loop(
                0, 32, functools.partial(tbody, count), (lo0, lo0, cand0, count(half, cand0)))
            thr_ref[0:half, :] = lo_a
            thr_ref[half:, :] = lo_b

    thr = thr_ref[...]

    keys = key_ref[:, 0:s]
    n_ge = _count(keys >= thr)
    dist = jnp.abs(qpos - col).astype(F32)
    has_tie = jnp.max(jnp.where((n_ge > kf) & (thr > NEG_KEY), 1.0, 0.0)) > 0.0

    cut_ref[...] = jnp.full((qt, 1), s, jnp.int32)

    @pl.when(has_tie)
    def _():
        eq = keys == thr
        need = kf - _count(keys > thr)

        def cbody(i, cut):
            cand = cut + (jnp.int32(1) << (11 - i))
            cnt = _count(eq & (col < cand))
            return jnp.where((cnt <= need) & (cand <= s), cand, cut)

        cut_ref[...] = lax.fori_loop(0, 12, cbody, jnp.zeros((qt, 1), jnp.int32))

    tied = jnp.where(col < cut_ref[...], dist, MASKED_DIST)
    d = jnp.where(keys > thr, dist, jnp.where(keys == thr, tied, MASKED_DIST))
    d_ref[:, 0:s] = jnp.where(admiss, d, MASKED_DIST)

    def qk(h, slot):
        lg_ref[slot, :, 0:s] = _dot_nt(q_ref[h], k_ref[0:s, :])

    def softmax(h, slot):
        lg = lg_ref[slot, :, 0:s] - slopes_ref[h] * d_ref[:, 0:s]
        p_ref[slot, :, 0:s] = jnp.exp2(lg - jnp.max(lg, axis=-1, keepdims=True)).astype(BF16)

    def pv(h, slot):
        lo = pl.multiple_of(h * HEAD_DIM, HEAD_DIM)
        o = _dot(p_ref[slot, :, 0:s], v_ref[0:s, :])
        o_ref[:, pl.ds(lo, HEAD_DIM)] = o[:, :HEAD_DIM] / o[:, HEAD_DIM:]

    qk(0, 0)
    qk(1, 1)
    softmax(0, 0)

    def hbody(t, carry):
        a = 2 * t
        pv(a, 0)
        softmax(a + 1, 1)
        qk(a + 2, 0)
        pv(a + 1, 1)
        softmax(a + 2, 0)
        qk(a + 3, 1)
        return carry

    lax.fori_loop(0, ATT_HEADS // 2 - 1, hbody, 0)
    pv(ATT_HEADS - 2, 0)
    softmax(ATT_HEADS - 1, 1)
    pv(ATT_HEADS - 1, 1)


def _dsa_kernel(slopes_ref, q_ref, k_ref, v_ref, qi_ref, ki_ref, wi_ref, o_ref,
                key_ref, d_ref, lg_ref, p_ref, cut_ref, thr_ref, *, seq, topk):
    jj = pl.program_id(1)
    n_steps = (Q_TILE * (jj + 1) + KEY_STEP - 1) // KEY_STEP
    for n in range(1, seq // KEY_STEP + 1):
        @pl.when(n_steps == n)
        def _(n=n):
            _dsa_block(n * KEY_STEP, jj, slopes_ref, q_ref, k_ref, v_ref, qi_ref, ki_ref, wi_ref, o_ref,
                       key_ref, d_ref, lg_ref, p_ref, cut_ref, thr_ref, topk=topk)


def _dsa(q, k, v, qi, ki, wi):
    nb, _, seq, _ = q.shape
    topk = min(TOPK_MAX, seq // 4)
    assert seq <= 2048 and seq % KEY_STEP == 0 and KEY_STEP % Q_TILE == 0 and Q_TILE % Q_BLOCK == 0 and CHUNK == 64
    assert ATT_HEADS % 2 == 0

    def qtile(w):
        return pl.BlockSpec((None, Q_TILE, w), lambda b, j: (b, j, 0))

    def full(w):
        return pl.BlockSpec((None, seq, w), lambda b, j: (b, 0, 0))

    return pl.pallas_call(
        functools.partial(_dsa_kernel, seq=seq, topk=topk),
        grid=(nb, seq // Q_TILE),
        in_specs=[pl.BlockSpec(memory_space=pltpu.SMEM),
                  pl.BlockSpec((None, ATT_HEADS, Q_TILE, HEAD_DIM), lambda b, j: (b, 0, j, 0)),
                  full(HEAD_DIM), full(2 * HEAD_DIM),
                  qtile(IDX_HEADS * IDX_DIM), full(IDX_DIM), qtile(IDX_HEADS)],
        out_specs=qtile(ATT_WIDTH),
        out_shape=jax.ShapeDtypeStruct((nb, seq, ATT_WIDTH), F32),
        scratch_shapes=[pltpu.VMEM((Q_TILE, seq), jnp.int32), pltpu.VMEM((Q_TILE, seq), F32),
                        pltpu.VMEM((2, Q_TILE, seq), F32), pltpu.VMEM((2, Q_TILE, seq), BF16),
                        pltpu.VMEM((Q_TILE, 1), jnp.int32), pltpu.VMEM((Q_TILE, 1), jnp.int32)],
        compiler_params=pltpu.CompilerParams(
            dimension_semantics=("arbitrary", "arbitrary"), vmem_limit_bytes=VMEM_LIMIT),
        name="dsa",
    )(jnp.asarray([_f32(sl * LOG2E) for sl in SLOPES], F32), q, k, v, qi, ki, wi)


def _b_out_kernel(x_ref, ymix_ref, w_ref, kv_ref, wout_ref, pg_ref, pb_ref, o_ref):
    x = x_ref[...]
    xb = x.astype(BF16)
    y_mem = _mem_attn(_dot(xb, w_ref[:, 0:MEM_WIDTH]), kv_ref)
    gate = _dot(xb, w_ref[:, MEM_WIDTH:])
    o_ref[...] = _gated_out(x, ymix_ref[...], y_mem, gate, wout_ref, pg_ref, pb_ref)


def _b_out(x, y_mix, layer, w_b, kv, w_out, pg, pb, *, tt=512):
    nb, seq, _ = x.shape
    assert B_PROJ_COLS == B_OUT_COLS
    return pl.pallas_call(
        _b_out_kernel,
        grid=(nb, seq // tt),
        in_specs=[
            pl.BlockSpec((None, tt, D_MODEL), lambda b, t: (b, t, 0)),
            pl.BlockSpec((None, tt, ATT_WIDTH), lambda b, t: (b, t, 0)),
            _half_spec(D_MODEL, B_OUT_COLS, 1),
            pl.BlockSpec((None, None, N_MEM, 2 * MEM_WIDTH), lambda b, t: (layer, b, 0, 0)),
            _const_spec((MIX_WIDTH, D_MODEL)),
            _const_spec((1, D_MODEL)),
            _const_spec((1, D_MODEL)),
        ],
        out_specs=pl.BlockSpec((None, tt, D_MODEL), lambda b, t: (b, t, 0)),
        out_shape=jax.ShapeDtypeStruct(x.shape, F32),
        compiler_params=pltpu.CompilerParams(
            dimension_semantics=("arbitrary", "arbitrary"), vmem_limit_bytes=VMEM_LIMIT),
        name="b_out",
    )(x, y_mix, w_b, kv, w_out, pg, pb)


def _pad_cols(w, n):
    return jnp.pad(w, ((0, 0), (0, n - w.shape[1])))


def kernel(x, mem, mem_ln_g, mem_ln_b, a_w_in, a_conv_w, a_conv_b, a_ln_g, a_ln_b, a_w_mkv, a_w_out,
           a_post_g, a_post_b, b_w_in, b_w_mkv, b_w_out, b_post_g, b_post_b):
    row = lambda v: v.reshape(1, -1)
    w_mkv_all = jnp.stack([(a_w_mkv if i % 2 == 0 else b_w_mkv)[i // 2] for i in range(DEPTH)]).astype(BF16)
    kv = _memkv(mem, row(mem_ln_g), row(mem_ln_b), w_mkv_all)

    c1 = ATT_WIDTH + 2 * HEAD_DIM + IDX_HEADS * IDX_DIM
    c2 = c1 + IDX_DIM
    c3 = c2 + IDX_HEADS

    for i in range(DEPTH):
        j = i // 2
        if i % 2 == 0:
            w = a_w_in[j].astype(BF16)
            nc = CONV_WIDTH // LANES
            w_ag = jnp.stack([w[:, :CONV_WIDTH].reshape(D_MODEL, nc, LANES),
                              w[:, CONV_WIDTH:2 * CONV_WIDTH].reshape(D_MODEL, nc, LANES)], axis=2)
            x = _layer_a(
                x, i, w_ag.reshape(D_MODEL, 2 * CONV_WIDTH), w[:, 2 * CONV_WIDTH:],
                jnp.pad(a_conv_w[j], ((0, 1), (0, 0))), row(a_conv_b[j]), row(a_ln_g[j]), row(a_ln_b[j]),
                kv, a_w_out[j].astype(BF16), row(a_post_g[j]), row(a_post_b[j]))
        else:
            w = b_w_in[j]
            w_b = jnp.concatenate(
                [w[:, :c1], _pad_cols(w[:, c1:c2], LANES), _pad_cols(w[:, c2:c3], LANES), w[:, c3:]],
                axis=1).astype(BF16)
            y_mix = _dsa(*_b_proj(x, w_b))
            x = _b_out(x, y_mix, i, w_b, kv, b_w_out[j].astype(BF16), row(b_post_g[j]), row(b_post_b[j]))
    return x
```

```python
import functools
import math

import numpy as np
import jax
import jax.numpy as jnp
from jax import lax
from jax.experimental import pallas as pl
from jax.experimental.pallas import tpu as pltpu

D_MODEL = 1024
DEPTH = 4
CHUNK = 64
Q_BLOCK = 128
HEAD_DIM = 128
N_MEM = 256
MEM_HEADS = 4
MEM_WIDTH = MEM_HEADS * HEAD_DIM
MIX_WIDTH = 2 * D_MODEL
CONV_WIDTH = MIX_WIDTH - MEM_WIDTH
CONV_KERNEL = 31
ATT_HEADS = CONV_WIDTH // HEAD_DIM
ATT_WIDTH = ATT_HEADS * HEAD_DIM
IDX_HEADS = 8
IDX_DIM = 64
TOPK_MAX = 256


def _f32(v):
    return float(np.float32(v))


ALPHA = _f32((2 * DEPTH) ** 0.25)
LN_EPS = _f32(1e-5)
NEG = _f32(-1e30)
SCALE = _f32(HEAD_DIM ** -0.5)
LOG2E = _f32(1.0 / math.log(2.0))

LANES = 128
CONV_HALO = 32
VMEM_LIMIT = 56 * 1024 * 1024

BF16 = jnp.bfloat16
F32 = jnp.float32


def _alibi_slopes(n):
    p = 2 ** int(math.floor(math.log2(n)))
    base = [2.0 ** (-8.0 * (i + 1) / p) for i in range(p)]
    extra = [2.0 ** (-4.0 * (2 * i + 1) / p) for i in range(n - p)]
    return [_f32(s) for s in base + extra]


SLOPES = _alibi_slopes(ATT_HEADS)


def _ln(x, g, b):
    mu = jnp.mean(x, axis=-1, keepdims=True)
    xc = x - mu
    var = jnp.mean(xc * xc, axis=-1, keepdims=True)
    return xc * lax.rsqrt(var + LN_EPS) * g + b


def _sigmoid(x):
    return 1.0 / (1.0 + jnp.exp(-x))


def _dot(a, b):
    return jnp.dot(a, b, preferred_element_type=F32)


def _dot_nt(a, b):
    return lax.dot_general(a, b, (((1,), (1,)), ((), ())), preferred_element_type=F32)


def _const_spec(shape):
    nd = len(shape)
    return pl.BlockSpec(shape, lambda *_: (0,) * nd, pipeline_mode=pl.Buffered(1))


def _memkv_kernel(mem_ref, g_ref, b_ref, w_ref, kv_ref):
    m = _ln(mem_ref[...], g_ref[...], b_ref[...])
    kv_ref[...] = _dot(m.astype(BF16), w_ref[...]).astype(BF16)


def _memkv(mem, g, b, w_all):
    nb = mem.shape[0]
    return pl.pallas_call(
        _memkv_kernel,
        grid=(DEPTH, nb),
        in_specs=[
            pl.BlockSpec((None, N_MEM, D_MODEL), lambda l, i: (i, 0, 0)),
            pl.BlockSpec((1, D_MODEL), lambda l, i: (0, 0)),
            pl.BlockSpec((1, D_MODEL), lambda l, i: (0, 0)),
            pl.BlockSpec((None, D_MODEL, 2 * MEM_WIDTH), lambda l, i: (l, 0, 0)),
        ],
        out_specs=pl.BlockSpec((None, None, N_MEM, 2 * MEM_WIDTH), lambda l, i: (l, i, 0, 0)),
        out_shape=jax.ShapeDtypeStruct((DEPTH, nb, N_MEM, 2 * MEM_WIDTH), BF16),
        compiler_params=pltpu.CompilerParams(dimension_semantics=("arbitrary", "arbitrary")),
        name="memkv",
    )(mem, g, b, w_all)


def _mem_attn(qm, kv_ref):
    outs = []
    for h in range(MEM_HEADS):
        lo = h * HEAD_DIM
        qh = (qm[:, lo:lo + HEAD_DIM] * SCALE).astype(BF16)
        kh = kv_ref[:, lo:lo + HEAD_DIM]
        vh = kv_ref[:, MEM_WIDTH + lo:MEM_WIDTH + lo + HEAD_DIM]
        s = _dot_nt(qh, kh)
        m = jnp.max(s, axis=-1, keepdims=True)
        p = jnp.exp(s - m)
        l = jnp.sum(p, axis=-1, keepdims=True)
        outs.append(_dot(p.astype(BF16), vh) / l)
    return outs


def _gated_out(x, y_mix, y_mem_heads, gate, wout_ref, pg_ref, pb_ref):
    sg = gate * _sigmoid(gate)
    acc = _dot((y_mix * sg[:, :CONV_WIDTH]).astype(BF16), wout_ref[0:CONV_WIDTH, :])
    for h, yh in enumerate(y_mem_heads):
        lo = CONV_WIDTH + h * HEAD_DIM
        acc = acc + _dot((yh * sg[:, lo:lo + HEAD_DIM]).astype(BF16), wout_ref[lo:lo + HEAD_DIM, :])
    return _ln(ALPHA * x + acc, pg_ref[...], pb_ref[...])


GLU_CHUNK = 2 * LANES


def _layer_a_kernel(x_ref, wag_ref, wgq_ref, cw_ref, cb_ref, lg_ref, lb_ref, kv_ref, wout_ref, pg_ref, pb_ref,
                    o_ref, xbuf, hbuf, cbuf, gqbuf, *, tt):
    t = pl.program_id(1)
    x = x_ref[...]
    xbuf[...] = x.astype(BF16)

    @pl.when(t == 0)
    def _():
        hbuf[0:CONV_HALO, :] = jnp.zeros((CONV_HALO, CONV_WIDTH), F32)

    @pl.when(t > 0)
    def _():
        hbuf[0:CONV_HALO, :] = hbuf[tt:tt + CONV_HALO, :]

    n_chunks = CONV_WIDTH // LANES
    n_gq = (MEM_WIDTH + MIX_WIDTH) // GLU_CHUNK
    off = CONV_HALO - (CONV_KERNEL - 1)
    half = tt // 2

    def glu_chunk(c):
        ag = _dot(xbuf[...], wag_ref[:, c * GLU_CHUNK:(c + 1) * GLU_CHUNK])
        hbuf[CONV_HALO:CONV_HALO + tt, c * LANES:(c + 1) * LANES] = ag[:, :LANES] * _sigmoid(ag[:, LANES:])

    def conv_chunk(c):
        lanes = slice(c * LANES, (c + 1) * LANES)
        rows = half + CONV_HALO
        for r0 in range(0, tt, half):
            strip = hbuf[r0:r0 + rows, lanes]
            acc = jnp.broadcast_to(cb_ref[:, lanes], (half, LANES))
            for phase in range(8):
                win = pltpu.roll(strip, rows - phase, axis=0) if phase else strip
                for a in range(CONV_HALO // 8 + 1):
                    k = 8 * a + phase - off
                    if 0 <= k < CONV_KERNEL:
                        acc = acc + win[8 * a:8 * a + half] * cw_ref[k:k + 1, lanes]
            cbuf[r0:r0 + half, lanes] = acc

    glu_chunk(0)
    for c in range(n_chunks):
        if c + 1 < n_chunks:
            glu_chunk(c + 1)
        if c < n_gq:
            gqbuf[:, c * GLU_CHUNK:(c + 1) * GLU_CHUNK] = _dot(xbuf[...], wgq_ref[:, c * GLU_CHUNK:(c + 1) * GLU_CHUNK])
        conv_chunk(c)

    yn = _ln(cbuf[...], lg_ref[...], lb_ref[...])
    y_mix = yn * _sigmoid(yn)
    y_mem = _mem_attn(gqbuf[:, 0:MEM_WIDTH], kv_ref)
    o_ref[...] = _gated_out(x, y_mix, y_mem, gqbuf[:, MEM_WIDTH:], wout_ref, pg_ref, pb_ref)


def _layer_a(x, layer, w_ag, w_gq, conv_w, conv_b, ln_g, ln_b, kv, w_out, pg, pb, *, tt=256):
    nb, seq, _ = x.shape
    assert (MEM_WIDTH + MIX_WIDTH) % GLU_CHUNK == 0 and (MEM_WIDTH + MIX_WIDTH) // GLU_CHUNK <= CONV_WIDTH // LANES
    return pl.pallas_call(
        functools.partial(_layer_a_kernel, tt=tt),
        grid=(nb, seq // tt),
        in_specs=[
            pl.BlockSpec((None, tt, D_MODEL), lambda b, t: (b, t, 0)),
            _const_spec(w_ag.shape),
            _const_spec(w_gq.shape),
            _const_spec((CONV_KERNEL + 1, CONV_WIDTH)),
            _const_spec((1, CONV_WIDTH)),
            _const_spec((1, CONV_WIDTH)),
            _const_spec((1, CONV_WIDTH)),
            pl.BlockSpec((None, None, N_MEM, 2 * MEM_WIDTH), lambda b, t: (layer, b, 0, 0)),
            _const_spec((MIX_WIDTH, D_MODEL)),
            _const_spec((1, D_MODEL)),
            _const_spec((1, D_MODEL)),
        ],
        out_specs=pl.BlockSpec((None, tt, D_MODEL), lambda b, t: (b, t, 0)),
        out_shape=jax.ShapeDtypeStruct(x.shape, F32),
        scratch_shapes=[
            pltpu.VMEM((tt, D_MODEL), BF16),
            pltpu.VMEM((tt + CONV_HALO, CONV_WIDTH), F32),
            pltpu.VMEM((tt, CONV_WIDTH), F32),
            pltpu.VMEM((tt, MEM_WIDTH + MIX_WIDTH), F32),
        ],
        compiler_params=pltpu.CompilerParams(
            dimension_semantics=("arbitrary", "arbitrary"), vmem_limit_bytes=VMEM_LIMIT),
        name="layer_a",
    )(x, w_ag, w_gq, conv_w, conv_b, ln_g, ln_b, kv, w_out, pg, pb)


B_PROJ_COLS = ATT_WIDTH + 2 * HEAD_DIM + IDX_HEADS * IDX_DIM + 2 * LANES
B_OUT_COLS = MEM_WIDTH + MIX_WIDTH


def _half_spec(rows, cols, blk):
    return pl.BlockSpec((rows, cols), lambda *_: (0, blk), pipeline_mode=pl.Buffered(1))


def _b_proj_kernel(x_ref, w_ref, q_ref, k_ref, v_ref, qi_ref, ki_ref, wi_ref):
    xb = x_ref[...].astype(BF16)
    nqkv = ATT_WIDTH + 2 * HEAD_DIM
    qkv = _dot(xb, w_ref[:, 0:nqkv])
    for h in range(ATT_HEADS):
        q_ref[h] = (qkv[:, h * HEAD_DIM:(h + 1) * HEAD_DIM] * (SCALE * LOG2E)).astype(BF16)
    k_ref[...] = qkv[:, ATT_WIDTH:ATT_WIDTH + HEAD_DIM].astype(BF16)
    v_ref[:, 0:HEAD_DIM] = qkv[:, ATT_WIDTH + HEAD_DIM:].astype(BF16)
    v_ref[:, HEAD_DIM:] = jnp.ones((v_ref.shape[0], HEAD_DIM), BF16)
    idx = _dot(xb, w_ref[:, nqkv:])
    nq = IDX_HEADS * IDX_DIM
    qi_ref[...] = idx[:, :nq].astype(BF16)
    ki_ref[...] = idx[:, nq:nq + IDX_DIM].astype(BF16)
    wi_ref[...] = idx[:, nq + LANES:nq + LANES + IDX_HEADS]


def _b_proj(x, w_b, *, tt=512):
    nb, seq, _ = x.shape
    nq = IDX_HEADS * IDX_DIM

    def tile(w):
        return pl.BlockSpec((None, tt, w), lambda b, t: (b, t, 0))

    return pl.pallas_call(
        _b_proj_kernel,
        grid=(nb, seq // tt),
        in_specs=[tile(D_MODEL), _half_spec(D_MODEL, B_PROJ_COLS, 0)],
        out_specs=[pl.BlockSpec((None, ATT_HEADS, tt, HEAD_DIM), lambda b, t: (b, 0, t, 0)),
                   tile(HEAD_DIM), tile(2 * HEAD_DIM), tile(nq), tile(IDX_DIM), tile(IDX_HEADS)],
        out_shape=[
            jax.ShapeDtypeStruct((nb, ATT_HEADS, seq, HEAD_DIM), BF16),
            jax.ShapeDtypeStruct((nb, seq, HEAD_DIM), BF16),
            jax.ShapeDtypeStruct((nb, seq, 2 * HEAD_DIM), BF16),
            jax.ShapeDtypeStruct((nb, seq, nq), BF16),
            jax.ShapeDtypeStruct((nb, seq, IDX_DIM), BF16),
            jax.ShapeDtypeStruct((nb, seq, IDX_HEADS), F32),
        ],
        compiler_params=pltpu.CompilerParams(
            dimension_semantics=("arbitrary", "arbitrary"), vmem_limit_bytes=VMEM_LIMIT),
        name="b_proj",
    )(x, w_b)


NEG_KEY = int(np.float32(NEG).view(np.int32)) ^ 0x7FFFFFFF
MASKED_DIST = _f32(-NEG / min(SLOPES))
Q_TILE = 256
KEY_STEP = 512
INT32_MIN = int(jnp.iinfo(jnp.int32).min)
CHUNK_SHIFT = CHUNK.bit_length() - 1


def _count(pred):
    return jnp.sum(jnp.where(pred, 1.0, 0.0), axis=-1, keepdims=True)


def _dsa_block(s, jj, slopes_ref, q_ref, k_ref, v_ref, qi_ref, ki_ref, wi_ref, o_ref,
               key_ref, d_ref, lg_ref, p_ref, cut_ref, thr_ref, *, topk):
    qt = Q_TILE
    row = lax.broadcasted_iota(jnp.int32, (qt, s), 0)
    col = lax.broadcasted_iota(jnp.int32, (qt, s), 1)
    qpos = row + jj * qt
    admiss = (col >> CHUNK_SHIFT) <= (qpos >> CHUNK_SHIFT)

    ki = ki_ref[0:s, :]
    wi = wi_ref[...]
    isc = jnp.zeros((qt, s), F32)
    for h in range(IDX_HEADS):
        sc = _dot_nt(qi_ref[:, h * IDX_DIM:(h + 1) * IDX_DIM], ki)
        isc = isc + wi[:, h:h + 1] * jnp.maximum(sc, 0.0)
    isc = jnp.where(admiss, isc, NEG) + 0.0

    bits = pltpu.bitcast(isc, jnp.int32)
    key_ref[:, 0:s] = jnp.where(bits < 0, bits ^ jnp.int32(0x7FFFFFFF), bits)

    kf = F32(topk)
    half = qt // 2

    def partial_count(ext, r0, cand):
        cb = jnp.broadcast_to(cand, (half, LANES))
        acc = jnp.zeros((half, LANES), F32)
        for t in range(ext // LANES):
            acc = acc + jnp.where(key_ref[r0:r0 + half, t * LANES:(t + 1) * LANES] >= cb, 1.0, 0.0)
        return acc

    def decide(part, cand, lo):
        return jnp.where(jnp.sum(part, axis=-1, keepdims=True) >= kf, cand, lo)

    def tbody(count, i, carry):
        lo_a, lo_b, cand_b, part_b = carry
        cand_a = lo_a + (jnp.int32(1) << (31 - i))
        part_a = count(0, cand_a)
        lo_b = decide(part_b, cand_b, lo_b)
        lo_a = decide(part_a, cand_a, lo_a)
        cand_b = lo_b + (jnp.int32(1) << jnp.maximum(30 - i, 0))
        return lo_a, lo_b, cand_b, count(half, cand_b)

    lo0 = jnp.full((half, 1), INT32_MIN, jnp.int32)
    cand0 = jnp.zeros((half, 1), jnp.int32)

    for ext in range(s, s - KEY_STEP, -qt):
        @pl.when(qt * (jj + 1) == ext)
        def _(ext=ext):
            count = functools.partial(partial_count, ext)
            lo_a, lo_b, _, _ = lax.fori_loop(
                0, 32, functools.partial(tbody, count), (lo0, lo0, cand0, count(half, cand0)))
            thr_ref[0:half, :] = lo_a
            thr_ref[half:, :] = lo_b

    thr = thr_ref[...]

    keys = key_ref[:, 0:s]
    n_ge = _count(keys >= thr)
    dist = jnp.abs(qpos - col).astype(F32)
    has_tie = jnp.max(jnp.where((n_ge > kf) & (thr > NEG_KEY), 1.0, 0.0)) > 0.0

    cut_ref[...] = jnp.full((qt, 1), s, jnp.int32)

    @pl.when(has_tie)
    def _():
        eq = keys == thr
        need = kf - _count(keys > thr)

        def cbody(i, cut):
            cand = cut + (jnp.int32(1) << (11 - i))
            cnt = _count(eq & (col < cand))
            return jnp.where((cnt <= need) & (cand <= s), cand, cut)

        cut_ref[...] = lax.fori_loop(0, 12, cbody, jnp.zeros((qt, 1), jnp.int32))

    tied = jnp.where(col < cut_ref[...], dist, MASKED_DIST)
    d = jnp.where(keys > thr, dist, jnp.where(keys == thr, tied, MASKED_DIST))
    d_ref[:, 0:s] = jnp.where(admiss, d, MASKED_DIST)

    def qk(h, slot):
        lg_ref[slot, :, 0:s] = _dot_nt(q_ref[h], k_ref[0:s, :])

    def softmax(h, slot):
        lg = lg_ref[slot, :, 0:s] - slopes_ref[h] * d_ref[:, 0:s]
        p_ref[slot, :, 0:s] = jnp.exp2(lg - jnp.max(lg, axis=-1, keepdims=True)).astype(BF16)

    def pv(h, slot):
        lo = pl.multiple_of(h * HEAD_DIM, HEAD_DIM)
        o = _dot(p_ref[slot, :, 0:s], v_ref[0:s, :])
        o_ref[:, pl.ds(lo, HEAD_DIM)] = o[:, :HEAD_DIM] / o[:, HEAD_DIM:]

    qk(0, 0)
    qk(1, 1)
    softmax(0, 0)

    def hbody(t, carry):
        a = 2 * t
        pv(a, 0)
        softmax(a + 1, 1)
        qk(a + 2, 0)
        pv(a + 1, 1)
        softmax(a + 2, 0)
        qk(a + 3, 1)
        return carry

    lax.fori_loop(0, ATT_HEADS // 2 - 1, hbody, 0)
    pv(ATT_HEADS - 2, 0)
    softmax(ATT_HEADS - 1, 1)
    pv(ATT_HEADS - 1, 1)


def _dsa_kernel(slopes_ref, q_ref, k_ref, v_ref, qi_ref, ki_ref, wi_ref, o_ref,
                key_ref, d_ref, lg_ref, p_ref, cut_ref, thr_ref, *, seq, topk):
    jj = pl.program_id(1)
    n_steps = (Q_TILE * (jj + 1) + KEY_STEP - 1) // KEY_STEP
    for n in range(1, seq // KEY_STEP + 1):
        @pl.when(n_steps == n)
        def _(n=n):
            _dsa_block(n * KEY_STEP, jj, slopes_ref, q_ref, k_ref, v_ref, qi_ref, ki_ref, wi_ref, o_ref,
                       key_ref, d_ref, lg_ref, p_ref, cut_ref, thr_ref, topk=topk)


def _dsa(q, k, v, qi, ki, wi):
    nb, _, seq, _ = q.shape
    topk = min(TOPK_MAX, seq // 4)
    assert seq <= 2048 and seq % KEY_STEP == 0 and KEY_STEP % Q_TILE == 0 and Q_TILE % Q_BLOCK == 0
    assert CHUNK == 1 << CHUNK_SHIFT
    assert ATT_HEADS % 2 == 0

    def qtile(w):
        return pl.BlockSpec((None, Q_TILE, w), lambda b, j: (b, j, 0))

    def full(w):
        return pl.BlockSpec((None, seq, w), lambda b, j: (b, 0, 0))

    return pl.pallas_call(
        functools.partial(_dsa_kernel, seq=seq, topk=topk),
        grid=(nb, seq // Q_TILE),
        in_specs=[pl.BlockSpec(memory_space=pltpu.SMEM),
                  pl.BlockSpec((None, ATT_HEADS, Q_TILE, HEAD_DIM), lambda b, j: (b, 0, j, 0)),
                  full(HEAD_DIM), full(2 * HEAD_DIM),
                  qtile(IDX_HEADS * IDX_DIM), full(IDX_DIM), qtile(IDX_HEADS)],
        out_specs=qtile(ATT_WIDTH),
        out_shape=jax.ShapeDtypeStruct((nb, seq, ATT_WIDTH), F32),
        scratch_shapes=[pltpu.VMEM((Q_TILE, seq), jnp.int32), pltpu.VMEM((Q_TILE, seq), F32),
                        pltpu.VMEM((2, Q_TILE, seq), F32), pltpu.VMEM((2, Q_TILE, seq), BF16),
                        pltpu.VMEM((Q_TILE, 1), jnp.int32), pltpu.VMEM((Q_TILE, 1), jnp.int32)],
        compiler_params=pltpu.CompilerParams(
            dimension_semantics=("arbitrary", "arbitrary"), vmem_limit_bytes=VMEM_LIMIT),
        name="dsa",
    )(jnp.asarray([_f32(sl * LOG2E) for sl in SLOPES], F32), q, k, v, qi, ki, wi)


def _b_out_kernel(x_ref, ymix_ref, w_ref, kv_ref, wout_ref, pg_ref, pb_ref, o_ref):
    x = x_ref[...]
    xb = x.astype(BF16)
    y_mem = _mem_attn(_dot(xb, w_ref[:, 0:MEM_WIDTH]), kv_ref)
    gate = _dot(xb, w_ref[:, MEM_WIDTH:])
    o_ref[...] = _gated_out(x, ymix_ref[...], y_mem, gate, wout_ref, pg_ref, pb_ref)


def _b_out(x, y_mix, layer, w_b, kv, w_out, pg, pb, *, tt=512):
    nb, seq, _ = x.shape
    assert B_PROJ_COLS == B_OUT_COLS
    return pl.pallas_call(
        _b_out_kernel,
        grid=(nb, seq // tt),
        in_specs=[
            pl.BlockSpec((None, tt, D_MODEL), lambda b, t: (b, t, 0)),
            pl.BlockSpec((None, tt, ATT_WIDTH), lambda b, t: (b, t, 0)),
            _half_spec(D_MODEL, B_OUT_COLS, 1),
            pl.BlockSpec((None, None, N_MEM, 2 * MEM_WIDTH), lambda b, t: (layer, b, 0, 0)),
            _const_spec((MIX_WIDTH, D_MODEL)),
            _const_spec((1, D_MODEL)),
            _const_spec((1, D_MODEL)),
        ],
        out_specs=pl.BlockSpec((None, tt, D_MODEL), lambda b, t: (b, t, 0)),
        out_shape=jax.ShapeDtypeStruct(x.shape, F32),
        compiler_params=pltpu.CompilerParams(
            dimension_semantics=("arbitrary", "arbitrary"), vmem_limit_bytes=VMEM_LIMIT),
        name="b_out",
    )(x, y_mix, w_b, kv, w_out, pg, pb)


def _pad_cols(w, n):
    return jnp.pad(w, ((0, 0), (0, n - w.shape[1])))


def kernel(x, mem, mem_ln_g, mem_ln_b, a_w_in, a_conv_w, a_conv_b, a_ln_g, a_ln_b, a_w_mkv, a_w_out,
           a_post_g, a_post_b, b_w_in, b_w_mkv, b_w_out, b_post_g, b_post_b):
    row = lambda v: v.reshape(1, -1)
    w_mkv_all = jnp.stack([(a_w_mkv if i % 2 == 0 else b_w_mkv)[i // 2] for i in range(DEPTH)]).astype(BF16)
    kv = _memkv(mem, row(mem_ln_g), row(mem_ln_b), w_mkv_all)

    c1 = ATT_WIDTH + 2 * HEAD_DIM + IDX_HEADS * IDX_DIM
    c2 = c1 + IDX_DIM
    c3 = c2 + IDX_HEADS

    for i in range(DEPTH):
        j = i // 2
        if i % 2 == 0:
            w = a_w_in[j].astype(BF16)
            nc = CONV_WIDTH // LANES
            w_ag = jnp.stack([w[:, :CONV_WIDTH].reshape(D_MODEL, nc, LANES),
                              w[:, CONV_WIDTH:2 * CONV_WIDTH].reshape(D_MODEL, nc, LANES)], axis=2)
            x = _layer_a(
                x, i, w_ag.reshape(D_MODEL, 2 * CONV_WIDTH), w[:, 2 * CONV_WIDTH:],
                jnp.pad(a_conv_w[j], ((0, 1), (0, 0))), row(a_conv_b[j]), row(a_ln_g[j]), row(a_ln_b[j]),
                kv, a_w_out[j].astype(BF16), row(a_post_g[j]), row(a_post_b[j]))
        else:
            w = b_w_in[j]
            w_b = jnp.concatenate(
                [w[:, :c1], _pad_cols(w[:, c1:c2], LANES), _pad_cols(w[:, c2:c3], LANES), w[:, c3:]],
                axis=1).astype(BF16)
            y_mix = _dsa(*_b_proj(x, w_b))
            x = _b_out(x, y_mix, i, w_b, kv, b_w_out[j].astype(BF16), row(b_post_g[j]), row(b_post_b[j]))
    return x
```

```python
import functools
import math

import numpy as np
import jax
import jax.numpy as jnp
from jax import lax
from jax.experimental import pallas as pl
from jax.experimental.pallas import tpu as pltpu

D_MODEL = 1024
DEPTH = 4
CHUNK = 64
Q_BLOCK = 128
HEAD_DIM = 128
N_MEM = 256
MEM_HEADS = 4
MEM_WIDTH = MEM_HEADS * HEAD_DIM
MIX_WIDTH = 2 * D_MODEL
CONV_WIDTH = MIX_WIDTH - MEM_WIDTH
CONV_KERNEL = 31
ATT_HEADS = CONV_WIDTH // HEAD_DIM
ATT_WIDTH = ATT_HEADS * HEAD_DIM
IDX_HEADS = 8
IDX_DIM = 64
TOPK_MAX = 256


def _f32(v):
    return float(np.float32(v))


ALPHA = _f32((2 * DEPTH) ** 0.25)
LN_EPS = _f32(1e-5)
NEG = _f32(-1e30)
SCALE = _f32(HEAD_DIM ** -0.5)
LOG2E = _f32(1.0 / math.log(2.0))

LANES = 128
CONV_ROWS = 128
CONV_HALO = 32
VMEM_LIMIT = 56 * 1024 * 1024

BF16 = jnp.bfloat16
F32 = jnp.float32


def _alibi_slopes(n):
    p = 2 ** int(math.floor(math.log2(n)))
    base = [2.0 ** (-8.0 * (i + 1) / p) for i in range(p)]
    extra = [2.0 ** (-4.0 * (2 * i + 1) / p) for i in range(n - p)]
    return [_f32(s) for s in base + extra]


SLOPES = _alibi_slopes(ATT_HEADS)


def _ln(x, g, b):
    mu = jnp.mean(x, axis=-1, keepdims=True)
    xc = x - mu
    var = jnp.mean(xc * xc, axis=-1, keepdims=True)
    return xc * lax.rsqrt(var + LN_EPS) * g + b


def _sigmoid(x):
    return 1.0 / (1.0 + jnp.exp2(x * (-LOG2E)))


def _dot(a, b):
    return jnp.dot(a, b, preferred_element_type=F32)


def _dot_nt(a, b):
    return lax.dot_general(a, b, (((1,), (1,)), ((), ())), preferred_element_type=F32)


def _const_spec(shape):
    nd = len(shape)
    return pl.BlockSpec(shape, lambda *_: (0,) * nd, pipeline_mode=pl.Buffered(1))


def _memkv_kernel(mem_ref, g_ref, b_ref, w_ref, kv_ref):
    m = _ln(mem_ref[...], g_ref[...], b_ref[...])
    kv_ref[...] = _dot(m.astype(BF16), w_ref[...]).astype(BF16)


def _memkv(mem, g, b, w_all):
    nb = mem.shape[0]
    return pl.pallas_call(
        _memkv_kernel,
        grid=(DEPTH, nb),
        in_specs=[
            pl.BlockSpec((None, N_MEM, D_MODEL), lambda l, i: (i, 0, 0)),
            pl.BlockSpec((1, D_MODEL), lambda l, i: (0, 0)),
            pl.BlockSpec((1, D_MODEL), lambda l, i: (0, 0)),
            pl.BlockSpec((None, D_MODEL, 2 * MEM_WIDTH), lambda l, i: (l, 0, 0)),
        ],
        out_specs=pl.BlockSpec((None, None, N_MEM, 2 * MEM_WIDTH), lambda l, i: (l, i, 0, 0)),
        out_shape=jax.ShapeDtypeStruct((DEPTH, nb, N_MEM, 2 * MEM_WIDTH), BF16),
        compiler_params=pltpu.CompilerParams(dimension_semantics=("arbitrary", "arbitrary")),
        name="memkv",
    )(mem, g, b, w_all)


def _mem_attn(qm, kv_ref):
    outs = []
    for h in range(MEM_HEADS):
        lo = h * HEAD_DIM
        qh = (qm[:, lo:lo + HEAD_DIM] * SCALE).astype(BF16)
        kh = kv_ref[:, lo:lo + HEAD_DIM]
        vh = kv_ref[:, MEM_WIDTH + lo:MEM_WIDTH + lo + HEAD_DIM]
        s = _dot_nt(qh, kh)
        m = jnp.max(s, axis=-1, keepdims=True)
        p = jnp.exp(s - m)
        l = jnp.sum(p, axis=-1, keepdims=True)
        outs.append(_dot(p.astype(BF16), vh) / l)
    return outs


def _gated_out(x, y_mix, y_mem_heads, gate, wout_ref, pg_ref, pb_ref):
    sg = gate * _sigmoid(gate)
    acc = _dot((y_mix * sg[:, :CONV_WIDTH]).astype(BF16), wout_ref[0:CONV_WIDTH, :])
    for h, yh in enumerate(y_mem_heads):
        lo = CONV_WIDTH + h * HEAD_DIM
        acc = acc + _dot((yh * sg[:, lo:lo + HEAD_DIM]).astype(BF16), wout_ref[lo:lo + HEAD_DIM, :])
    return _ln(ALPHA * x + acc, pg_ref[...], pb_ref[...])


GLU_CHUNK = 2 * LANES


def _layer_a_kernel(x_ref, wag_ref, wgq_ref, cw_ref, cb_ref, lg_ref, lb_ref, kv_ref, wout_ref, pg_ref, pb_ref,
                    o_ref, xbuf, hbuf, cbuf, gqbuf, *, tt):
    t = pl.program_id(1)
    x = x_ref[...]
    xbuf[...] = x.astype(BF16)

    @pl.when(t == 0)
    def _():
        hbuf[0:CONV_HALO, :] = jnp.zeros((CONV_HALO, CONV_WIDTH), F32)

    @pl.when(t > 0)
    def _():
        hbuf[0:CONV_HALO, :] = hbuf[tt:tt + CONV_HALO, :]

    n_chunks = CONV_WIDTH // LANES
    n_gq = (MEM_WIDTH + MIX_WIDTH) // GLU_CHUNK
    off = CONV_HALO - (CONV_KERNEL - 1)
    half = CONV_ROWS

    def glu_chunk(c):
        ag = _dot(xbuf[...], wag_ref[:, c * GLU_CHUNK:(c + 1) * GLU_CHUNK])
        hbuf[CONV_HALO:CONV_HALO + tt, c * LANES:(c + 1) * LANES] = ag[:, :LANES] * _sigmoid(ag[:, LANES:])

    def conv_chunk(c):
        lanes = slice(c * LANES, (c + 1) * LANES)
        rows = half + CONV_HALO
        for r0 in range(0, tt, half):
            strip = hbuf[r0:r0 + rows, lanes]
            acc = jnp.broadcast_to(cb_ref[:, lanes], (half, LANES))
            for phase in range(8):
                win = pltpu.roll(strip, rows - phase, axis=0) if phase else strip
                for a in range(CONV_HALO // 8 + 1):
                    k = 8 * a + phase - off
                    if 0 <= k < CONV_KERNEL:
                        acc = acc + win[8 * a:8 * a + half] * cw_ref[k:k + 1, lanes]
            cbuf[r0:r0 + half, lanes] = acc

    glu_chunk(0)
    for c in range(n_chunks):
        if c + 1 < n_chunks:
            glu_chunk(c + 1)
        if c < n_gq:
            gqbuf[:, c * GLU_CHUNK:(c + 1) * GLU_CHUNK] = _dot(xbuf[...], wgq_ref[:, c * GLU_CHUNK:(c + 1) * GLU_CHUNK])
        conv_chunk(c)

    yn = _ln(cbuf[...], lg_ref[...], lb_ref[...])
    y_mix = yn * _sigmoid(yn)
    y_mem = _mem_attn(gqbuf[:, 0:MEM_WIDTH], kv_ref)
    o_ref[...] = _gated_out(x, y_mix, y_mem, gqbuf[:, MEM_WIDTH:], wout_ref, pg_ref, pb_ref)


def _layer_a(x, layer, w_ag, w_gq, conv_w, conv_b, ln_g, ln_b, kv, w_out, pg, pb, *, tt=512):
    nb, seq, _ = x.shape
    assert (MEM_WIDTH + MIX_WIDTH) % GLU_CHUNK == 0 and (MEM_WIDTH + MIX_WIDTH) // GLU_CHUNK <= CONV_WIDTH // LANES
    return pl.pallas_call(
        functools.partial(_layer_a_kernel, tt=tt),
        grid=(nb, seq // tt),
        in_specs=[
            pl.BlockSpec((None, tt, D_MODEL), lambda b, t: (b, t, 0)),
            _const_spec(w_ag.shape),
            _const_spec(w_gq.shape),
            _const_spec((CONV_KERNEL + 1, CONV_WIDTH)),
            _const_spec((1, CONV_WIDTH)),
            _const_spec((1, CONV_WIDTH)),
            _const_spec((1, CONV_WIDTH)),
            pl.BlockSpec((None, None, N_MEM, 2 * MEM_WIDTH), lambda b, t: (layer, b, 0, 0)),
            _const_spec((MIX_WIDTH, D_MODEL)),
            _const_spec((1, D_MODEL)),
            _const_spec((1, D_MODEL)),
        ],
        out_specs=pl.BlockSpec((None, tt, D_MODEL), lambda b, t: (b, t, 0)),
        out_shape=jax.ShapeDtypeStruct(x.shape, F32),
        scratch_shapes=[
            pltpu.VMEM((tt, D_MODEL), BF16),
            pltpu.VMEM((tt + CONV_HALO, CONV_WIDTH), F32),
            pltpu.VMEM((tt, CONV_WIDTH), F32),
            pltpu.VMEM((tt, MEM_WIDTH + MIX_WIDTH), F32),
        ],
        compiler_params=pltpu.CompilerParams(
            dimension_semantics=("arbitrary", "arbitrary"), vmem_limit_bytes=VMEM_LIMIT),
        name="layer_a",
    )(x, w_ag, w_gq, conv_w, conv_b, ln_g, ln_b, kv, w_out, pg, pb)


B_PROJ_COLS = ATT_WIDTH + 2 * HEAD_DIM + IDX_HEADS * IDX_DIM + 2 * LANES
B_OUT_COLS = MEM_WIDTH + MIX_WIDTH


def _half_spec(rows, cols, blk):
    return pl.BlockSpec((rows, cols), lambda *_: (0, blk), pipeline_mode=pl.Buffered(1))


def _b_proj_kernel(x_ref, w_ref, q_ref, k_ref, v_ref, qi_ref, ki_ref, wi_ref):
    xb = x_ref[...].astype(BF16)
    nqkv = ATT_WIDTH + 2 * HEAD_DIM
    qkv = _dot(xb, w_ref[:, 0:nqkv])
    for h in range(ATT_HEADS):
        q_ref[h] = (qkv[:, h * HEAD_DIM:(h + 1) * HEAD_DIM] * (SCALE * LOG2E)).astype(BF16)
    k_ref[...] = qkv[:, ATT_WIDTH:ATT_WIDTH + HEAD_DIM].astype(BF16)
    v_ref[:, 0:HEAD_DIM] = qkv[:, ATT_WIDTH + HEAD_DIM:].astype(BF16)
    v_ref[:, HEAD_DIM:] = jnp.ones((v_ref.shape[0], HEAD_DIM), BF16)
    idx = _dot(xb, w_ref[:, nqkv:])
    nq = IDX_HEADS * IDX_DIM
    qi_ref[...] = idx[:, :nq].astype(BF16)
    ki_ref[...] = idx[:, nq:nq + IDX_DIM].astype(BF16)
    wi_ref[...] = idx[:, nq + LANES:nq + LANES + IDX_HEADS]


def _b_proj(x, w_b, *, tt=1024):
    nb, seq, _ = x.shape
    nq = IDX_HEADS * IDX_DIM

    def tile(w):
        return pl.BlockSpec((None, tt, w), lambda b, t: (b, t, 0))

    return pl.pallas_call(
        _b_proj_kernel,
        grid=(nb, seq // tt),
        in_specs=[tile(D_MODEL), _half_spec(D_MODEL, B_PROJ_COLS, 0)],
        out_specs=[pl.BlockSpec((None, ATT_HEADS, tt, HEAD_DIM), lambda b, t: (b, 0, t, 0)),
                   tile(HEAD_DIM), tile(2 * HEAD_DIM), tile(nq), tile(IDX_DIM), tile(IDX_HEADS)],
        out_shape=[
            jax.ShapeDtypeStruct((nb, ATT_HEADS, seq, HEAD_DIM), BF16),
            jax.ShapeDtypeStruct((nb, seq, HEAD_DIM), BF16),
            jax.ShapeDtypeStruct((nb, seq, 2 * HEAD_DIM), BF16),
            jax.ShapeDtypeStruct((nb, seq, nq), BF16),
            jax.ShapeDtypeStruct((nb, seq, IDX_DIM), BF16),
            jax.ShapeDtypeStruct((nb, seq, IDX_HEADS), F32),
        ],
        compiler_params=pltpu.CompilerParams(
            dimension_semantics=("arbitrary", "arbitrary"), vmem_limit_bytes=VMEM_LIMIT),
        name="b_proj",
    )(x, w_b)


NEG_KEY = int(np.float32(NEG).view(np.int32)) ^ 0x7FFFFFFF
MASKED_DIST = _f32(-NEG / min(SLOPES))
Q_TILE = 256
KEY_STEP = 512
INT32_MIN = int(jnp.iinfo(jnp.int32).min)
CHUNK_SHIFT = CHUNK.bit_length() - 1


def _count(pred):
    return jnp.sum(jnp.where(pred, 1.0, 0.0), axis=-1, keepdims=True)


def _dsa_block(s, jj, slopes_ref, q_ref, k_ref, v_ref, qi_ref, ki_ref, wi_ref, o_ref,
               key_ref, d_ref, lg_ref, p_ref, cut_ref, thr_ref, *, topk):
    qt = Q_TILE
    row = lax.broadcasted_iota(jnp.int32, (qt, s), 0)
    col = lax.broadcasted_iota(jnp.int32, (qt, s), 1)
    qpos = row + jj * qt
    admiss = (col >> CHUNK_SHIFT) <= (qpos >> CHUNK_SHIFT)

    ki = ki_ref[0:s, :]
    wi = wi_ref[...]
    isc = jnp.zeros((qt, s), F32)
    for h in range(IDX_HEADS):
        sc = _dot_nt(qi_ref[:, h * IDX_DIM:(h + 1) * IDX_DIM], ki)
        isc = isc + wi[:, h:h + 1] * jnp.maximum(sc, 0.0)
    isc = jnp.where(admiss, isc, NEG) + 0.0

    bits = pltpu.bitcast(isc, jnp.int32)
    key_ref[:, 0:s] = jnp.where(bits < 0, bits ^ jnp.int32(0x7FFFFFFF), bits)

    kf = F32(topk)
    half = qt // 2

    def partial_count(ext, r0, cand):
        cb = jnp.broadcast_to(cand, (half, LANES))
        acc = jnp.zeros((half, LANES), F32)
        for t in range(ext // LANES):
            acc = acc + jnp.where(key_ref[r0:r0 + half, t * LANES:(t + 1) * LANES] >= cb, 1.0, 0.0)
        return acc

    def decide(part, cand, lo):
        return jnp.where(jnp.sum(part, axis=-1, keepdims=True) >= kf, cand, lo)

    def tbody(count, i, carry):
        lo_a, lo_b, cand_b, part_b = carry
        cand_a = lo_a + (jnp.int32(1) << (31 - i))
        part_a = count(0, cand_a)
        lo_b = decide(part_b, cand_b, lo_b)
        lo_a = decide(part_a, cand_a, lo_a)
        cand_b = lo_b + (jnp.int32(1) << jnp.maximum(30 - i, 0))
        return lo_a, lo_b, cand_b, count(half, cand_b)

    lo0 = jnp.full((half, 1), INT32_MIN, jnp.int32)
    cand0 = jnp.zeros((half, 1), jnp.int32)

    for ext in range(s, s - KEY_STEP, -qt):
        @pl.when(qt * (jj + 1) == ext)
        def _(ext=ext):
            count = functools.partial(partial_count, ext)
            lo_a, lo_b, _, _ = lax.fori_loop(
                0, 32, functools.partial(tbody, count), (lo0, lo0, cand0, count(half, cand0)))
            thr_ref[0:half, :] = lo_a
            thr_ref[half:, :] = lo_b

    thr = thr_ref[...]

    keys = key_ref[:, 0:s]
    n_ge = _count(keys >= thr)
    dist = jnp.abs(qpos - col).astype(F32)
    has_tie = jnp.max(jnp.where((n_ge > kf) & (thr > NEG_KEY), 1.0, 0.0)) > 0.0

    cut_ref[...] = jnp.full((qt, 1), s, jnp.int32)

    @pl.when(has_tie)
    def _():
        eq = keys == thr
        need = kf - _count(keys > thr)

        def cbody(i, cut):
            cand = cut + (jnp.int32(1) << (11 - i))
            cnt = _count(eq & (col < cand))
            return jnp.where((cnt <= need) & (cand <= s), cand, cut)

        cut_ref[...] = lax.fori_loop(0, 12, cbody, jnp.zeros((qt, 1), jnp.int32))

    tied = jnp.where(col < cut_ref[...], dist, MASKED_DIST)
    d = jnp.where(keys > thr, dist, jnp.where(keys == thr, tied, MASKED_DIST))
    d_ref[:, 0:s] = jnp.where(admiss, d, MASKED_DIST)

    def qk(h, slot):
        lg_ref[slot, :, 0:s] = _dot_nt(q_ref[h], k_ref[0:s, :])

    def softmax(h, slot):
        lg = lg_ref[slot, :, 0:s] - slopes_ref[h] * d_ref[:, 0:s]
        p_ref[slot, :, 0:s] = jnp.exp2(lg - jnp.max(lg, axis=-1, keepdims=True)).astype(BF16)

    def pv(h, slot):
        lo = pl.multiple_of(h * HEAD_DIM, HEAD_DIM)
        o = _dot(p_ref[slot, :, 0:s], v_ref[0:s, :])
        o_ref[:, pl.ds(lo, HEAD_DIM)] = o[:, :HEAD_DIM] / o[:, HEAD_DIM:]

    qk(0, 0)
    qk(1, 1)
    softmax(0, 0)

    def hbody(t, carry):
        a = 2 * t
        pv(a, 0)
        softmax(a + 1, 1)
        qk(a + 2, 0)
        pv(a + 1, 1)
        softmax(a + 2, 0)
        qk(a + 3, 1)
        return carry

    lax.fori_loop(0, ATT_HEADS // 2 - 1, hbody, 0)
    pv(ATT_HEADS - 2, 0)
    softmax(ATT_HEADS - 1, 1)
    pv(ATT_HEADS - 1, 1)


def _dsa_kernel(slopes_ref, q_ref, k_ref, v_ref, qi_ref, ki_ref, wi_ref, o_ref,
                key_ref, d_ref, lg_ref, p_ref, cut_ref, thr_ref, *, seq, topk):
    jj = pl.program_id(1)
    n_steps = (Q_TILE * (jj + 1) + KEY_STEP - 1) // KEY_STEP
    for n in range(1, seq // KEY_STEP + 1):
        @pl.when(n_steps == n)
        def _(n=n):
            _dsa_block(n * KEY_STEP, jj, slopes_ref, q_ref, k_ref, v_ref, qi_ref, ki_ref, wi_ref, o_ref,
                       key_ref, d_ref, lg_ref, p_ref, cut_ref, thr_ref, topk=topk)


def _dsa(q, k, v, qi, ki, wi):
    nb, _, seq, _ = q.shape
    topk = min(TOPK_MAX, seq // 4)
    assert seq <= 2048 and seq % KEY_STEP == 0 and KEY_STEP % Q_TILE == 0 and Q_TILE % Q_BLOCK == 0
    assert CHUNK == 1 << CHUNK_SHIFT
    assert ATT_HEADS % 2 == 0

    def qtile(w):
        return pl.BlockSpec((None, Q_TILE, w), lambda b, j: (b, j, 0))

    def full(w):
        return pl.BlockSpec((None, seq, w), lambda b, j: (b, 0, 0))

    return pl.pallas_call(
        functools.partial(_dsa_kernel, seq=seq, topk=topk),
        grid=(nb, seq // Q_TILE),
        in_specs=[pl.BlockSpec(memory_space=pltpu.SMEM),
                  pl.BlockSpec((None, ATT_HEADS, Q_TILE, HEAD_DIM), lambda b, j: (b, 0, j, 0)),
                  full(HEAD_DIM), full(2 * HEAD_DIM),
                  qtile(IDX_HEADS * IDX_DIM), full(IDX_DIM), qtile(IDX_HEADS)],
        out_specs=qtile(ATT_WIDTH),
        out_shape=jax.ShapeDtypeStruct((nb, seq, ATT_WIDTH), F32),
        scratch_shapes=[pltpu.VMEM((Q_TILE, seq), jnp.int32), pltpu.VMEM((Q_TILE, seq), F32),
                        pltpu.VMEM((2, Q_TILE, seq), F32), pltpu.VMEM((2, Q_TILE, seq), BF16),
                        pltpu.VMEM((Q_TILE, 1), jnp.int32), pltpu.VMEM((Q_TILE, 1), jnp.int32)],
        compiler_params=pltpu.CompilerParams(
            dimension_semantics=("arbitrary", "arbitrary"), vmem_limit_bytes=VMEM_LIMIT),
        name="dsa",
    )(jnp.asarray([_f32(sl * LOG2E) for sl in SLOPES], F32), q, k, v, qi, ki, wi)


def _b_out_kernel(x_ref, ymix_ref, w_ref, kv_ref, wout_ref, pg_ref, pb_ref, o_ref):
    x = x_ref[...]
    xb = x.astype(BF16)
    y_mem = _mem_attn(_dot(xb, w_ref[:, 0:MEM_WIDTH]), kv_ref)
    gate = _dot(xb, w_ref[:, MEM_WIDTH:])
    o_ref[...] = _gated_out(x, ymix_ref[...], y_mem, gate, wout_ref, pg_ref, pb_ref)


def _b_out(x, y_mix, layer, w_b, kv, w_out, pg, pb, *, tt=512):
    nb, seq, _ = x.shape
    assert B_PROJ_COLS == B_OUT_COLS
    return pl.pallas_call(
        _b_out_kernel,
        grid=(nb, seq // tt),
        in_specs=[
            pl.BlockSpec((None, tt, D_MODEL), lambda b, t: (b, t, 0)),
            pl.BlockSpec((None, tt, ATT_WIDTH), lambda b, t: (b, t, 0)),
            _half_spec(D_MODEL, B_OUT_COLS, 1),
            pl.BlockSpec((None, None, N_MEM, 2 * MEM_WIDTH), lambda b, t: (layer, b, 0, 0)),
            _const_spec((MIX_WIDTH, D_MODEL)),
            _const_spec((1, D_MODEL)),
            _const_spec((1, D_MODEL)),
        ],
        out_specs=pl.BlockSpec((None, tt, D_MODEL), lambda b, t: (b, t, 0)),
        out_shape=jax.ShapeDtypeStruct(x.shape, F32),
        compiler_params=pltpu.CompilerParams(
            dimension_semantics=("arbitrary", "arbitrary"), vmem_limit_bytes=VMEM_LIMIT),
        name="b_out",
    )(x, y_mix, w_b, kv, w_out, pg, pb)


def _pad_cols(w, n):
    return jnp.pad(w, ((0, 0), (0, n - w.shape[1])))


def kernel(x, mem, mem_ln_g, mem_ln_b, a_w_in, a_conv_w, a_conv_b, a_ln_g, a_ln_b, a_w_mkv, a_w_out,
           a_post_g, a_post_b, b_w_in, b_w_mkv, b_w_out, b_post_g, b_post_b):
    row = lambda v: v.reshape(1, -1)
    w_mkv_all = jnp.stack([(a_w_mkv if i % 2 == 0 else b_w_mkv)[i // 2] for i in range(DEPTH)]).astype(BF16)
    kv = _memkv(mem, row(mem_ln_g), row(mem_ln_b), w_mkv_all)

    c1 = ATT_WIDTH + 2 * HEAD_DIM + IDX_HEADS * IDX_DIM
    c2 = c1 + IDX_DIM
    c3 = c2 + IDX_HEADS

    for i in range(DEPTH):
        j = i // 2
        if i % 2 == 0:
            w = a_w_in[j].astype(BF16)
            nc = CONV_WIDTH // LANES
            w_ag = jnp.stack([w[:, :CONV_WIDTH].reshape(D_MODEL, nc, LANES),
                              w[:, CONV_WIDTH:2 * CONV_WIDTH].reshape(D_MODEL, nc, LANES)], axis=2)
            x = _layer_a(
                x, i, w_ag.reshape(D_MODEL, 2 * CONV_WIDTH), w[:, 2 * CONV_WIDTH:],
                jnp.pad(a_conv_w[j], ((0, 1), (0, 0))), row(a_conv_b[j]), row(a_ln_g[j]), row(a_ln_b[j]),
                kv, a_w_out[j].astype(BF16), row(a_post_g[j]), row(a_post_b[j]))
        else:
            w = b_w_in[j]
            w_b = jnp.concatenate(
                [w[:, :c1], _pad_cols(w[:, c1:c2], LANES), _pad_cols(w[:, c2:c3], LANES), w[:, c3:]],
                axis=1).astype(BF16)
            y_mix = _dsa(*_b_proj(x, w_b))
            x = _b_out(x, y_mix, i, w_b, kv, b_w_out[j].astype(BF16), row(b_post_g[j]), row(b_post_b[j]))
    return x
```

```python
import functools
import math

import numpy as np
import jax
import jax.numpy as jnp
from jax import lax
from jax.experimental import pallas as pl
from jax.experimental.pallas import tpu as pltpu

D_MODEL = 1024
DEPTH = 4
CHUNK = 64
Q_BLOCK = 128
HEAD_DIM = 128
N_MEM = 256
MEM_HEADS = 4
MEM_WIDTH = MEM_HEADS * HEAD_DIM
MIX_WIDTH = 2 * D_MODEL
CONV_WIDTH = MIX_WIDTH - MEM_WIDTH
CONV_KERNEL = 31
ATT_HEADS = CONV_WIDTH // HEAD_DIM
ATT_WIDTH = ATT_HEADS * HEAD_DIM
IDX_HEADS = 8
IDX_DIM = 64
TOPK_MAX = 256


def _f32(v):
    return float(np.float32(v))


ALPHA = _f32((2 * DEPTH) ** 0.25)
LN_EPS = _f32(1e-5)
NEG = _f32(-1e30)
SCALE = _f32(HEAD_DIM ** -0.5)
LOG2E = _f32(1.0 / math.log(2.0))

LANES = 128
CONV_ROWS = 128
CONV_HALO = 32
VMEM_LIMIT = 56 * 1024 * 1024

BF16 = jnp.bfloat16
F32 = jnp.float32


def _alibi_slopes(n):
    p = 2 ** int(math.floor(math.log2(n)))
    base = [2.0 ** (-8.0 * (i + 1) / p) for i in range(p)]
    extra = [2.0 ** (-4.0 * (2 * i + 1) / p) for i in range(n - p)]
    return [_f32(s) for s in base + extra]


SLOPES = _alibi_slopes(ATT_HEADS)


def _ln(x, g, b):
    mu = jnp.mean(x, axis=-1, keepdims=True)
    xc = x - mu
    var = jnp.mean(xc * xc, axis=-1, keepdims=True)
    return xc * lax.rsqrt(var + LN_EPS) * g + b


def _sigmoid(x):
    return 1.0 / (1.0 + jnp.exp2(x * (-LOG2E)))


def _dot(a, b):
    return jnp.dot(a, b, preferred_element_type=F32)


def _dot_nt(a, b):
    return lax.dot_general(a, b, (((1,), (1,)), ((), ())), preferred_element_type=F32)


def _const_spec(shape):
    nd = len(shape)
    return pl.BlockSpec(shape, lambda *_: (0,) * nd, pipeline_mode=pl.Buffered(1))


def _memkv_kernel(mem_ref, g_ref, b_ref, w_ref, kv_ref):
    m = _ln(mem_ref[...], g_ref[...], b_ref[...]).astype(BF16)
    for layer in range(DEPTH):
        kv_ref[layer] = _dot(m, w_ref[layer]).astype(BF16)


def _memkv(mem, g, b, w_all):
    nb = mem.shape[0]
    return pl.pallas_call(
        _memkv_kernel,
        grid=(nb,),
        in_specs=[
            pl.BlockSpec((None, N_MEM, D_MODEL), lambda i: (i, 0, 0)),
            _const_spec((1, D_MODEL)),
            _const_spec((1, D_MODEL)),
            _const_spec((DEPTH, D_MODEL, 2 * MEM_WIDTH)),
        ],
        out_specs=pl.BlockSpec((DEPTH, None, N_MEM, 2 * MEM_WIDTH), lambda i: (0, i, 0, 0)),
        out_shape=jax.ShapeDtypeStruct((DEPTH, nb, N_MEM, 2 * MEM_WIDTH), BF16),
        compiler_params=pltpu.CompilerParams(dimension_semantics=("arbitrary",), vmem_limit_bytes=VMEM_LIMIT),
        name="memkv",
    )(mem, g, b, w_all)


def _mem_attn(qm, kv_ref):
    outs = []
    for h in range(MEM_HEADS):
        lo = h * HEAD_DIM
        qh = (qm[:, lo:lo + HEAD_DIM] * SCALE).astype(BF16)
        kh = kv_ref[:, lo:lo + HEAD_DIM]
        vh = kv_ref[:, MEM_WIDTH + lo:MEM_WIDTH + lo + HEAD_DIM]
        s = _dot_nt(qh, kh)
        m = jnp.max(s, axis=-1, keepdims=True)
        p = jnp.exp(s - m)
        l = jnp.sum(p, axis=-1, keepdims=True)
        outs.append(_dot(p.astype(BF16), vh) / l)
    return outs


def _gated_out(x, y_mix, y_mem_heads, gate, wout_ref, pg_ref, pb_ref):
    sg = gate * _sigmoid(gate)
    acc = _dot((y_mix * sg[:, :CONV_WIDTH]).astype(BF16), wout_ref[0:CONV_WIDTH, :])
    for h, yh in enumerate(y_mem_heads):
        lo = CONV_WIDTH + h * HEAD_DIM
        acc = acc + _dot((yh * sg[:, lo:lo + HEAD_DIM]).astype(BF16), wout_ref[lo:lo + HEAD_DIM, :])
    return _ln(ALPHA * x + acc, pg_ref[...], pb_ref[...])


GLU_CHUNK = 2 * LANES


def _layer_a_kernel(x_ref, wag_ref, wgq_ref, cw_ref, cb_ref, lg_ref, lb_ref, kv_ref, wout_ref, pg_ref, pb_ref,
                    o_ref, xbuf, hbuf, cbuf, gqbuf, *, tt):
    t = pl.program_id(1)
    x = x_ref[...]
    xbuf[...] = x.astype(BF16)

    @pl.when(t == 0)
    def _():
        hbuf[0:CONV_HALO, :] = jnp.zeros((CONV_HALO, CONV_WIDTH), F32)

    @pl.when(t > 0)
    def _():
        hbuf[0:CONV_HALO, :] = hbuf[tt:tt + CONV_HALO, :]

    n_chunks = CONV_WIDTH // LANES
    n_gq = (MEM_WIDTH + MIX_WIDTH) // GLU_CHUNK
    off = CONV_HALO - (CONV_KERNEL - 1)
    half = CONV_ROWS

    def glu_chunk(c):
        ag = _dot(xbuf[...], wag_ref[:, c * GLU_CHUNK:(c + 1) * GLU_CHUNK])
        hbuf[CONV_HALO:CONV_HALO + tt, c * LANES:(c + 1) * LANES] = ag[:, :LANES] * _sigmoid(ag[:, LANES:])

    def conv_chunk(c):
        lanes = slice(c * LANES, (c + 1) * LANES)
        rows = half + CONV_HALO
        for r0 in range(0, tt, half):
            strip = hbuf[r0:r0 + rows, lanes]
            acc = jnp.broadcast_to(cb_ref[:, lanes], (half, LANES))
            for phase in range(8):
                win = pltpu.roll(strip, rows - phase, axis=0) if phase else strip
                for a in range(CONV_HALO // 8 + 1):
                    k = 8 * a + phase - off
                    if 0 <= k < CONV_KERNEL:
                        acc = acc + win[8 * a:8 * a + half] * cw_ref[k:k + 1, lanes]
            cbuf[r0:r0 + half, lanes] = acc

    glu_chunk(0)
    for c in range(n_chunks):
        if c + 1 < n_chunks:
            glu_chunk(c + 1)
        if c < n_gq:
            gqbuf[:, c * GLU_CHUNK:(c + 1) * GLU_CHUNK] = _dot(xbuf[...], wgq_ref[:, c * GLU_CHUNK:(c + 1) * GLU_CHUNK])
        conv_chunk(c)

    yn = _ln(cbuf[...], lg_ref[...], lb_ref[...])
    y_mix = yn * _sigmoid(yn)
    y_mem = _mem_attn(gqbuf[:, 0:MEM_WIDTH], kv_ref)
    o_ref[...] = _gated_out(x, y_mix, y_mem, gqbuf[:, MEM_WIDTH:], wout_ref, pg_ref, pb_ref)


def _layer_a(x, layer, w_ag, w_gq, conv_w, conv_b, ln_g, ln_b, kv, w_out, pg, pb, *, tt=512):
    nb, seq, _ = x.shape
    assert (MEM_WIDTH + MIX_WIDTH) % GLU_CHUNK == 0 and (MEM_WIDTH + MIX_WIDTH) // GLU_CHUNK <= CONV_WIDTH // LANES
    return pl.pallas_call(
        functools.partial(_layer_a_kernel, tt=tt),
        grid=(nb, seq // tt),
        in_specs=[
            pl.BlockSpec((None, tt, D_MODEL), lambda b, t: (b, t, 0)),
            _const_spec(w_ag.shape),
            _const_spec(w_gq.shape),
            _const_spec((CONV_KERNEL + 1, CONV_WIDTH)),
            _const_spec((1, CONV_WIDTH)),
            _const_spec((1, CONV_WIDTH)),
            _const_spec((1, CONV_WIDTH)),
            pl.BlockSpec((None, None, N_MEM, 2 * MEM_WIDTH), lambda b, t: (layer, b, 0, 0)),
            _const_spec((MIX_WIDTH, D_MODEL)),
            _const_spec((1, D_MODEL)),
            _const_spec((1, D_MODEL)),
        ],
        out_specs=pl.BlockSpec((None, tt, D_MODEL), lambda b, t: (b, t, 0)),
        out_shape=jax.ShapeDtypeStruct(x.shape, F32),
        scratch_shapes=[
            pltpu.VMEM((tt, D_MODEL), BF16),
            pltpu.VMEM((tt + CONV_HALO, CONV_WIDTH), F32),
            pltpu.VMEM((tt, CONV_WIDTH), F32),
            pltpu.VMEM((tt, MEM_WIDTH + MIX_WIDTH), F32),
        ],
        compiler_params=pltpu.CompilerParams(
            dimension_semantics=("arbitrary", "arbitrary"), vmem_limit_bytes=VMEM_LIMIT),
        name="layer_a",
    )(x, w_ag, w_gq, conv_w, conv_b, ln_g, ln_b, kv, w_out, pg, pb)


B_PROJ_COLS = ATT_WIDTH + 2 * HEAD_DIM + IDX_HEADS * IDX_DIM + 2 * LANES
B_OUT_COLS = MEM_WIDTH + MIX_WIDTH


def _half_spec(rows, cols, blk):
    return pl.BlockSpec((rows, cols), lambda *_: (0, blk), pipeline_mode=pl.Buffered(1))


def _b_proj_kernel(x_ref, w_ref, q_ref, k_ref, v_ref, qi_ref, ki_ref, wi_ref):
    xb = x_ref[...].astype(BF16)
    nqkv = ATT_WIDTH + 2 * HEAD_DIM
    qkv = _dot(xb, w_ref[:, 0:nqkv])
    for h in range(ATT_HEADS):
        q_ref[h] = (qkv[:, h * HEAD_DIM:(h + 1) * HEAD_DIM] * (SCALE * LOG2E)).astype(BF16)
    k_ref[...] = qkv[:, ATT_WIDTH:ATT_WIDTH + HEAD_DIM].astype(BF16)
    v_ref[:, 0:HEAD_DIM] = qkv[:, ATT_WIDTH + HEAD_DIM:].astype(BF16)
    v_ref[:, HEAD_DIM:] = jnp.ones((v_ref.shape[0], HEAD_DIM), BF16)
    idx = _dot(xb, w_ref[:, nqkv:])
    nq = IDX_HEADS * IDX_DIM
    qi_ref[...] = idx[:, :nq].astype(BF16)
    ki_ref[...] = idx[:, nq:nq + IDX_DIM].astype(BF16)
    wi_ref[...] = idx[:, nq + LANES:nq + LANES + IDX_HEADS]


def _b_proj(x, w_b, *, tt=1024):
    nb, seq, _ = x.shape
    nq = IDX_HEADS * IDX_DIM

    def tile(w):
        return pl.BlockSpec((None, tt, w), lambda b, t: (b, t, 0))

    return pl.pallas_call(
        _b_proj_kernel,
        grid=(nb, seq // tt),
        in_specs=[tile(D_MODEL), _half_spec(D_MODEL, B_PROJ_COLS, 0)],
        out_specs=[pl.BlockSpec((None, ATT_HEADS, tt, HEAD_DIM), lambda b, t: (b, 0, t, 0)),
                   tile(HEAD_DIM), tile(2 * HEAD_DIM), tile(nq), tile(IDX_DIM), tile(IDX_HEADS)],
        out_shape=[
            jax.ShapeDtypeStruct((nb, ATT_HEADS, seq, HEAD_DIM), BF16),
            jax.ShapeDtypeStruct((nb, seq, HEAD_DIM), BF16),
            jax.ShapeDtypeStruct((nb, seq, 2 * HEAD_DIM), BF16),
            jax.ShapeDtypeStruct((nb, seq, nq), BF16),
            jax.ShapeDtypeStruct((nb, seq, IDX_DIM), BF16),
            jax.ShapeDtypeStruct((nb, seq, IDX_HEADS), F32),
        ],
        compiler_params=pltpu.CompilerParams(
            dimension_semantics=("arbitrary", "arbitrary"), vmem_limit_bytes=VMEM_LIMIT),
        name="b_proj",
    )(x, w_b)


NEG_KEY = int(np.float32(NEG).view(np.int32)) ^ 0x7FFFFFFF
MASKED_DIST = _f32(-NEG / min(SLOPES))
Q_TILE = 256
KEY_STEP = 512
INT32_MIN = int(jnp.iinfo(jnp.int32).min)
CHUNK_SHIFT = CHUNK.bit_length() - 1


def _count(pred):
    return jnp.sum(jnp.where(pred, 1.0, 0.0), axis=-1, keepdims=True)


def _dsa_block(s, jj, slopes_ref, q_ref, k_ref, v_ref, qi_ref, ki_ref, wi_ref, o_ref,
               key_ref, d_ref, lg_ref, p_ref, cut_ref, thr_ref, *, topk):
    qt = Q_TILE
    row = lax.broadcasted_iota(jnp.int32, (qt, s), 0)
    col = lax.broadcasted_iota(jnp.int32, (qt, s), 1)
    qpos = row + jj * qt
    admiss = (col >> CHUNK_SHIFT) <= (qpos >> CHUNK_SHIFT)

    ki = ki_ref[0:s, :]
    wi = wi_ref[...]
    isc = jnp.zeros((qt, s), F32)
    for h in range(IDX_HEADS):
        sc = _dot_nt(qi_ref[:, h * IDX_DIM:(h + 1) * IDX_DIM], ki)
        isc = isc + wi[:, h:h + 1] * jnp.maximum(sc, 0.0)
    isc = jnp.where(admiss, isc, NEG) + 0.0

    bits = pltpu.bitcast(isc, jnp.int32)
    key_ref[:, 0:s] = jnp.where(bits < 0, bits ^ jnp.int32(0x7FFFFFFF), bits)

    kf = F32(topk)
    half = qt // 2

    def partial_count(ext, r0, cand):
        cb = jnp.broadcast_to(cand, (half, LANES))
        acc = jnp.zeros((half, LANES), F32)
        for t in range(ext // LANES):
            acc = acc + jnp.where(key_ref[r0:r0 + half, t * LANES:(t + 1) * LANES] >= cb, 1.0, 0.0)
        return acc

    def decide(part, cand, lo):
        return jnp.where(jnp.sum(part, axis=-1, keepdims=True) >= kf, cand, lo)

    def tbody(count, i, carry):
        lo_a, lo_b, cand_b, part_b = carry
        cand_a = lo_a + (jnp.int32(1) << (31 - i))
        part_a = count(0, cand_a)
        lo_b = decide(part_b, cand_b, lo_b)
        lo_a = decide(part_a, cand_a, lo_a)
        cand_b = lo_b + (jnp.int32(1) << jnp.maximum(30 - i, 0))
        return lo_a, lo_b, cand_b, count(half, cand_b)

    lo0 = jnp.full((half, 1), INT32_MIN, jnp.int32)
    cand0 = jnp.zeros((half, 1), jnp.int32)

    for ext in range(s, s - KEY_STEP, -qt):
        @pl.when(qt * (jj + 1) == ext)
        def _(ext=ext):
            count = functools.partial(partial_count, ext)
            lo_a, lo_b, _, _ = lax.fori_loop(
                0, 32, functools.partial(tbody, count), (lo0, lo0, cand0, count(half, cand0)))
            thr_ref[0:half, :] = lo_a
            thr_ref[half:, :] = lo_b

    thr = thr_ref[...]

    keys = key_ref[:, 0:s]
    n_ge = _count(keys >= thr)
    dist = jnp.abs(qpos - col).astype(F32)
    has_tie = jnp.max(jnp.where((n_ge > kf) & (thr > NEG_KEY), 1.0, 0.0)) > 0.0

    cut_ref[...] = jnp.full((qt, 1), s, jnp.int32)

    @pl.when(has_tie)
    def _():
        eq = keys == thr
        need = kf - _count(keys > thr)

        def cbody(i, cut):
            cand = cut + (jnp.int32(1) << (11 - i))
            cnt = _count(eq & (col < cand))
            return jnp.where((cnt <= need) & (cand <= s), cand, cut)

        cut_ref[...] = lax.fori_loop(0, 12, cbody, jnp.zeros((qt, 1), jnp.int32))

    tied = jnp.where(col < cut_ref[...], dist, MASKED_DIST)
    d = jnp.where(keys > thr, dist, jnp.where(keys == thr, tied, MASKED_DIST))
    d_ref[:, 0:s] = jnp.where(admiss, d, MASKED_DIST)

    def qk(h, slot):
        lg_ref[slot, :, 0:s] = _dot_nt(q_ref[h], k_ref[0:s, :])

    def softmax(h, slot):
        lg = lg_ref[slot, :, 0:s] - slopes_ref[h] * d_ref[:, 0:s]
        p_ref[slot, :, 0:s] = jnp.exp2(lg - jnp.max(lg, axis=-1, keepdims=True)).astype(BF16)

    def pv(h, slot):
        lo = pl.multiple_of(h * HEAD_DIM, HEAD_DIM)
        o = _dot(p_ref[slot, :, 0:s], v_ref[0:s, :])
        o_ref[:, pl.ds(lo, HEAD_DIM)] = o[:, :HEAD_DIM] / o[:, HEAD_DIM:]

    qk(0, 0)
    qk(1, 1)
    softmax(0, 0)

    def hbody(t, carry):
        a = 2 * t
        pv(a, 0)
        softmax(a + 1, 1)
        qk(a + 2, 0)
        pv(a + 1, 1)
        softmax(a + 2, 0)
        qk(a + 3, 1)
        return carry

    lax.fori_loop(0, ATT_HEADS // 2 - 1, hbody, 0)
    pv(ATT_HEADS - 2, 0)
    softmax(ATT_HEADS - 1, 1)
    pv(ATT_HEADS - 1, 1)


def _dsa_kernel(slopes_ref, q_ref, k_ref, v_ref, qi_ref, ki_ref, wi_ref, o_ref,
                key_ref, d_ref, lg_ref, p_ref, cut_ref, thr_ref, *, seq, topk):
    jj = pl.program_id(1)
    n_steps = (Q_TILE * (jj + 1) + KEY_STEP - 1) // KEY_STEP
    for n in range(1, seq // KEY_STEP + 1):
        @pl.when(n_steps == n)
        def _(n=n):
            _dsa_block(n * KEY_STEP, jj, slopes_ref, q_ref, k_ref, v_ref, qi_ref, ki_ref, wi_ref, o_ref,
                       key_ref, d_ref, lg_ref, p_ref, cut_ref, thr_ref, topk=topk)


def _dsa(q, k, v, qi, ki, wi):
    nb, _, seq, _ = q.shape
    topk = min(TOPK_MAX, seq // 4)
    assert seq <= 2048 and seq % KEY_STEP == 0 and KEY_STEP % Q_TILE == 0 and Q_TILE % Q_BLOCK == 0
    assert CHUNK == 1 << CHUNK_SHIFT
    assert ATT_HEADS % 2 == 0

    def qtile(w):
        return pl.BlockSpec((None, Q_TILE, w), lambda b, j: (b, j, 0))

    def full(w):
        return pl.BlockSpec((None, seq, w), lambda b, j: (b, 0, 0))

    return pl.pallas_call(
        functools.partial(_dsa_kernel, seq=seq, topk=topk),
        grid=(nb, seq // Q_TILE),
        in_specs=[pl.BlockSpec(memory_space=pltpu.SMEM),
                  pl.BlockSpec((None, ATT_HEADS, Q_TILE, HEAD_DIM), lambda b, j: (b, 0, j, 0)),
                  full(HEAD_DIM), full(2 * HEAD_DIM),
                  qtile(IDX_HEADS * IDX_DIM), full(IDX_DIM), qtile(IDX_HEADS)],
        out_specs=qtile(ATT_WIDTH),
        out_shape=jax.ShapeDtypeStruct((nb, seq, ATT_WIDTH), F32),
        scratch_shapes=[pltpu.VMEM((Q_TILE, seq), jnp.int32), pltpu.VMEM((Q_TILE, seq), F32),
                        pltpu.VMEM((2, Q_TILE, seq), F32), pltpu.VMEM((2, Q_TILE, seq), BF16),
                        pltpu.VMEM((Q_TILE, 1), jnp.int32), pltpu.VMEM((Q_TILE, 1), jnp.int32)],
        compiler_params=pltpu.CompilerParams(
            dimension_semantics=("arbitrary", "arbitrary"), vmem_limit_bytes=VMEM_LIMIT),
        name="dsa",
    )(jnp.asarray([_f32(sl * LOG2E) for sl in SLOPES], F32), q, k, v, qi, ki, wi)


def _b_out_kernel(x_ref, ymix_ref, w_ref, kv_ref, wout_ref, pg_ref, pb_ref, o_ref):
    x = x_ref[...]
    xb = x.astype(BF16)
    y_mem = _mem_attn(_dot(xb, w_ref[:, 0:MEM_WIDTH]), kv_ref)
    gate = _dot(xb, w_ref[:, MEM_WIDTH:])
    o_ref[...] = _gated_out(x, ymix_ref[...], y_mem, gate, wout_ref, pg_ref, pb_ref)


def _b_out(x, y_mix, layer, w_b, kv, w_out, pg, pb, *, tt=1024):
    nb, seq, _ = x.shape
    assert B_PROJ_COLS == B_OUT_COLS
    return pl.pallas_call(
        _b_out_kernel,
        grid=(nb, seq // tt),
        in_specs=[
            pl.BlockSpec((None, tt, D_MODEL), lambda b, t: (b, t, 0)),
            pl.BlockSpec((None, tt, ATT_WIDTH), lambda b, t: (b, t, 0)),
            _half_spec(D_MODEL, B_OUT_COLS, 1),
            pl.BlockSpec((None, None, N_MEM, 2 * MEM_WIDTH), lambda b, t: (layer, b, 0, 0)),
            _const_spec((MIX_WIDTH, D_MODEL)),
            _const_spec((1, D_MODEL)),
            _const_spec((1, D_MODEL)),
        ],
        out_specs=pl.BlockSpec((None, tt, D_MODEL), lambda b, t: (b, t, 0)),
        out_shape=jax.ShapeDtypeStruct(x.shape, F32),
        compiler_params=pltpu.CompilerParams(
            dimension_semantics=("arbitrary", "arbitrary"), vmem_limit_bytes=VMEM_LIMIT),
        name="b_out",
    )(x, y_mix, w_b, kv, w_out, pg, pb)


def _pad_cols(w, n):
    return jnp.pad(w, ((0, 0), (0, n - w.shape[1])))


def kernel(x, mem, mem_ln_g, mem_ln_b, a_w_in, a_conv_w, a_conv_b, a_ln_g, a_ln_b, a_w_mkv, a_w_out,
           a_post_g, a_post_b, b_w_in, b_w_mkv, b_w_out, b_post_g, b_post_b):
    row = lambda v: v.reshape(1, -1)
    w_mkv_all = jnp.stack([(a_w_mkv if i % 2 == 0 else b_w_mkv)[i // 2] for i in range(DEPTH)]).astype(BF16)
    kv = _memkv(mem, row(mem_ln_g), row(mem_ln_b), w_mkv_all)

    c1 = ATT_WIDTH + 2 * HEAD_DIM + IDX_HEADS * IDX_DIM
    c2 = c1 + IDX_DIM
    c3 = c2 + IDX_HEADS

    for i in range(DEPTH):
        j = i // 2
        if i % 2 == 0:
            w = a_w_in[j].astype(BF16)
            nc = CONV_WIDTH // LANES
            w_ag = jnp.stack([w[:, :CONV_WIDTH].reshape(D_MODEL, nc, LANES),
                              w[:, CONV_WIDTH:2 * CONV_WIDTH].reshape(D_MODEL, nc, LANES)], axis=2)
            x = _layer_a(
                x, i, w_ag.reshape(D_MODEL, 2 * CONV_WIDTH), w[:, 2 * CONV_WIDTH:],
                jnp.pad(a_conv_w[j], ((0, 1), (0, 0))), row(a_conv_b[j]), row(a_ln_g[j]), row(a_ln_b[j]),
                kv, a_w_out[j].astype(BF16), row(a_post_g[j]), row(a_post_b[j]))
        else:
            w = b_w_in[j]
            w_b = jnp.concatenate(
                [w[:, :c1], _pad_cols(w[:, c1:c2], LANES), _pad_cols(w[:, c2:c3], LANES), w[:, c3:]],
                axis=1).astype(BF16)
            y_mix = _dsa(*_b_proj(x, w_b))
            x = _b_out(x, y_mix, i, w_b, kv, b_w_out[j].astype(BF16), row(b_post_g[j]), row(b_post_b[j]))
    return x
```
